```python
import math
import jax, jax.numpy as jnp
from jax import lax
import numpy as np

D_MODEL = 1024
BATCH = 1
SEQ = 16384
DEPTH = 4

ATTN_HEADS = 4
HEAD_DIM = 64
V_DIM = 2 * HEAD_DIM
ATTN_WIDTH = ATTN_HEADS * V_DIM
QK_WIDTH = ATTN_HEADS * 2 * HEAD_DIM
POOL_WINDOWS = (2, 4, 8, 16)
POOL_GROUPS = len(POOL_WINDOWS)
POOL_GROUP_DIM = 128
POOL_WIDTH = POOL_GROUPS * POOL_GROUP_DIM
N_BRANCHES = 2
GATE_WIDTH = N_BRANCHES * D_MODEL
IN_COLS = 2 * QK_WIDTH + ATTN_WIDTH + POOL_WIDTH + GATE_WIDTH
D_FF = 4 * D_MODEL
BLOCK_Q = 128
EPS = 1e-6
NEG_BIG = -1e30

kernel_name = "hybrid_diffattn_pool_gated_block"


def rms_norm(x, g):
    xf = x.astype(jnp.float32)
    y = xf * lax.rsqrt(jnp.mean(xf * xf, axis=-1, keepdims=True) + EPS)
    return (y * g.astype(jnp.float32)).astype(x.dtype)


def lambda_init_fn(layer_idx):
    return 0.8 - 0.6 * math.exp(-0.3 * layer_idx)


def diff_attention(q, k, v, lam):
    B, S, H, _, d = q.shape
    nb = S // BLOCK_Q
    qh = q.transpose(0, 2, 3, 1, 4)
    kh = k.transpose(0, 2, 3, 1, 4)
    vh = v.transpose(0, 2, 1, 3)
    qb = qh.reshape(B, H, 2, nb, BLOCK_Q, d).transpose(3, 0, 1, 2, 4, 5)
    k_pos = jnp.arange(S)
    scale = d ** -0.5

    def one_block(args):
        q_blk, start = args
        s = jnp.einsum('bhcqd,bhckd->bhcqk', q_blk, kh).astype(jnp.float32) * scale
        q_pos = start + jnp.arange(BLOCK_Q)
        mask = k_pos[None, :] <= q_pos[:, None]
        s = jnp.where(mask, s, NEG_BIG)
        p = jax.nn.softmax(s, axis=-1)
        w = p[:, :, 0] - lam * p[:, :, 1]
        return jnp.einsum('bhqk,bhke->bhqe', w.astype(vh.dtype), vh)

    starts = jnp.arange(nb) * BLOCK_Q
    out = lax.map(one_block, (qb, starts))
    return out.transpose(1, 0, 3, 2, 4).reshape(B, S, H, 2 * d)


def causal_pool_mixer(xp, pool_w, pool_scale):
    B, S, _ = xp.shape
    xg = xp.reshape(B, S, POOL_GROUPS, POOL_GROUP_DIM)
    xf = xg.astype(jnp.float32)
    c0 = jnp.concatenate([jnp.zeros((B, 1, POOL_GROUPS, POOL_GROUP_DIM), jnp.float32),
                          jnp.cumsum(xf, axis=1)], axis=1)
    upper = c0[:, 1:]
    t1 = jnp.arange(1, S + 1, dtype=jnp.float32)
    outs = []
    for g, w in enumerate(POOL_WINDOWS):
        lower = jnp.concatenate([jnp.zeros((B, w - 1, POOL_GROUP_DIM), jnp.float32),
                                 c0[:, :S - w + 1, g]], axis=1)
        count = jnp.minimum(t1, float(w))[None, :, None]
        outs.append((upper[:, :, g] - lower) / count - xf[:, :, g])
    pooled = jnp.stack(outs, axis=2).astype(xp.dtype)
    mixed = jnp.einsum('bsgc,gcd->bsgd', pooled, pool_w)
    return mixed.reshape(B, S, POOL_WIDTH) * pool_scale


def setup_inputs(seed: int = 0) -> dict:
    key = jax.random.key(seed)
    ks = jax.random.split(key, 20)
    f = jnp.float32
    nrm = lambda k, shape, s: jax.random.normal(k, shape, f) * s
    return {
        "x": jax.random.normal(ks[0], (BATCH, SEQ, D_MODEL), f),
        "norm1_g": 1.0 + nrm(ks[1], (DEPTH, D_MODEL), 0.1),
        "w_in": nrm(ks[2], (DEPTH, D_MODEL, IN_COLS), D_MODEL ** -0.5),
        "b_gate": nrm(ks[3], (DEPTH, GATE_WIDTH), 0.1),
        "q_norm_g": 1.0 + nrm(ks[4], (DEPTH, HEAD_DIM), 0.1),
        "k_norm_g": 1.0 + nrm(ks[5], (DEPTH, HEAD_DIM), 0.1),
        "lam_params": nrm(ks[6], (DEPTH, 4, HEAD_DIM), 0.1),
        "subln_g": 1.0 + nrm(ks[7], (DEPTH, V_DIM), 0.1),
        "pool_w": nrm(ks[8], (DEPTH, POOL_GROUPS, POOL_GROUP_DIM, POOL_GROUP_DIM), POOL_GROUP_DIM ** -0.5),
        "pool_scale": 1.0 + nrm(ks[9], (DEPTH, POOL_WIDTH), 0.1),
        "w_up_attn": nrm(ks[10], (DEPTH, ATTN_WIDTH, D_MODEL), ATTN_WIDTH ** -0.5),
        "w_up_pool": nrm(ks[11], (DEPTH, POOL_WIDTH, D_MODEL), POOL_WIDTH ** -0.5),
        "w_o": nrm(ks[12], (DEPTH, D_MODEL, D_MODEL), D_MODEL ** -0.5),
        "norm2_g": 1.0 + nrm(ks[13], (DEPTH, D_MODEL), 0.1),
        "w_mlp_in": nrm(ks[14], (DEPTH, D_MODEL, D_FF), D_MODEL ** -0.5),
        "w_mlp_out": nrm(ks[15], (DEPTH, D_FF, D_MODEL), D_FF ** -0.5),
    }


def reference(x, norm1_g, w_in, b_gate, q_norm_g, k_norm_g, lam_params, subln_g,
              pool_w, pool_scale, w_up_attn, w_up_pool, w_o, norm2_g, w_mlp_in, w_mlp_out):
    B, S, _ = x.shape
    c_q = QK_WIDTH
    c_k = c_q + QK_WIDTH
    c_v = c_k + ATTN_WIDTH
    c_p = c_v + POOL_WIDTH
    for l in range(DEPTH):
        lam_init = lambda_init_fn(l)
        h = rms_norm(x, norm1_g[l])
        proj = jnp.einsum('bsd,dc->bsc', h, w_in[l])
        q = proj[..., :c_q].reshape(B, S, ATTN_HEADS, 2, HEAD_DIM)
        k = proj[..., c_q:c_k].reshape(B, S, ATTN_HEADS, 2, HEAD_DIM)
        v = proj[..., c_k:c_v].reshape(B, S, ATTN_HEADS, V_DIM)
        xp = proj[..., c_v:c_p]
        gates = jax.nn.sigmoid(proj[..., c_p:] + b_gate[l]).reshape(B, S, N_BRANCHES, D_MODEL)

        q = rms_norm(q, q_norm_g[l])
        k = rms_norm(k, k_norm_g[l])
        lp = lam_params[l].astype(jnp.float32)
        lam = (jnp.exp(jnp.sum(lp[0] * lp[1])) - jnp.exp(jnp.sum(lp[2] * lp[3]))
               + lam_init)
        a = diff_attention(q, k, v, lam)
        a = rms_norm(a, subln_g[l]) * (1.0 - lam_init)
        y_attn = jnp.einsum('bsa,ad->bsd', a.reshape(B, S, ATTN_WIDTH), w_up_attn[l])

        p = causal_pool_mixer(xp, pool_w[l], pool_scale[l])
        y_pool = jnp.einsum('bsp,pd->bsd', p, w_up_pool[l])

        merged = gates[:, :, 0] * y_attn + gates[:, :, 1] * y_pool
        x = x + jnp.einsum('bsd,de->bse', merged, w_o[l])

        h2 = rms_norm(x, norm2_g[l])
        u = jnp.square(jax.nn.relu(jnp.einsum('bsd,df->bsf', h2, w_mlp_in[l])))
        x = x + jnp.einsum('bsf,fd->bsd', u, w_mlp_out[l])
    return x
```

```python
import functools
import math

import jax
import jax.numpy as jnp
from jax import lax
from jax.experimental import pallas as pl
from jax.experimental.pallas import tpu as pltpu

D_MODEL = 1024
DEPTH = 4
ATTN_HEADS = 4
HEAD_DIM = 64
V_DIM = 2 * HEAD_DIM
ATTN_WIDTH = ATTN_HEADS * V_DIM
QK_WIDTH = ATTN_HEADS * 2 * HEAD_DIM
POOL_WINDOWS = (2, 4, 8, 16)
POOL_GROUPS = len(POOL_WINDOWS)
POOL_GROUP_DIM = 128
POOL_WIDTH = POOL_GROUPS * POOL_GROUP_DIM
GATE_WIDTH = 2 * D_MODEL
IN_COLS = 2 * QK_WIDTH + ATTN_WIDTH + POOL_WIDTH + GATE_WIDTH
D_FF = 4 * D_MODEL
EPS = 1e-6
NEG_BIG = -1e30

POOL_HALO = 16
ROW_TILE = 512
ATTN_TQ = 256
ATTN_TK = 512
FF_CHUNK = 1024
VMEM_LIMIT = 56 * 1024 * 1024

_BF16 = jnp.bfloat16
_F32 = jnp.float32


def _const_spec(shape):
    return pl.BlockSpec(shape, lambda *_: (0,) * len(shape), pipeline_mode=pl.Buffered(1))


def _rms_scale(x):
    return lax.rsqrt(jnp.mean(x * x, axis=-1, keepdims=True) + EPS)


def _proj_kernel(x_ref, g1_ref, w_ref, bg_ref, gq_ref, gk_ref, seg_ref,
                 q_ref, k_ref, vt_ref, xp_ref, gate_ref):
    x = x_ref[...]
    h = (x * _rms_scale(x) * g1_ref[...]).astype(_BF16)

    def cols(lo, hi):
        return jnp.dot(h, w_ref[:, lo:hi], preferred_element_type=_F32)

    def qk_norm(t, g_ref):
        ms = jnp.dot((t * t).astype(_BF16), seg_ref[...], preferred_element_type=_F32)
        return (t * lax.rsqrt(ms + EPS) * g_ref[...]).astype(_BF16)

    c_q, c_k = QK_WIDTH, 2 * QK_WIDTH
    c_v, c_p = c_k + ATTN_WIDTH, c_k + ATTN_WIDTH + POOL_WIDTH
    q_ref[...] = qk_norm(cols(0, c_q), gq_ref)
    k_ref[...] = qk_norm(cols(c_q, c_k), gk_ref)
    v = cols(c_k, c_v)
    for hd in range(ATTN_HEADS):
        vt_ref[hd * V_DIM:(hd + 1) * V_DIM, :] = v[:, hd * V_DIM:(hd + 1) * V_DIM].T.astype(_BF16)
    xp_ref[...] = cols(c_v, c_p)
    gate_ref[...] = jax.nn.sigmoid(cols(c_p, IN_COLS) + bg_ref[...]).astype(_BF16)


def _proj(x, g1, w_in, b_gate, gq, gk, seg):
    s = x.shape[0]
    tm = ROW_TILE
    row = lambda w: pl.BlockSpec((tm, w), lambda i: (i, 0))
    return pl.pallas_call(
        _proj_kernel,
        grid=(s // tm,),
        in_specs=[row(D_MODEL), _const_spec((1, D_MODEL)), _const_spec((D_MODEL, IN_COLS)),
                  _const_spec((1, GATE_WIDTH)), _const_spec((1, QK_WIDTH)), _const_spec((1, QK_WIDTH)),
                  _const_spec((QK_WIDTH, QK_WIDTH))],
        out_specs=[row(QK_WIDTH), row(QK_WIDTH), pl.BlockSpec((ATTN_WIDTH, tm), lambda i: (0, i)),
                   row(POOL_WIDTH), row(GATE_WIDTH)],
        out_shape=[jax.ShapeDtypeStruct((s, QK_WIDTH), _BF16), jax.ShapeDtypeStruct((s, QK_WIDTH), _BF16),
                   jax.ShapeDtypeStruct((ATTN_WIDTH, s), _BF16), jax.ShapeDtypeStruct((s, POOL_WIDTH), _F32),
                   jax.ShapeDtypeStruct((s, GATE_WIDTH), _BF16)],
        compiler_params=pltpu.CompilerParams(dimension_semantics=("arbitrary",),
                                             vmem_limit_bytes=VMEM_LIMIT),
        name="proj",
    )(x, g1, w_in, b_gate, gq, gk, seg)


def _attn_kernel(lam0_ref, lp_ref, gs_ref, q_ref, k_ref, vt_ref, o_ref,
                 qs_ref, m_ref, l_ref, acc_ref):
    tq, tk = ATTN_TQ, ATTN_TK
    i = pl.program_id(1)

    q = q_ref[...]
    lane = lax.broadcasted_iota(jnp.int32, q.shape, 1)
    zero = jnp.zeros_like(q)
    qs_ref[0:tq, :] = jnp.where(lane < HEAD_DIM, q, zero)
    qs_ref[tq:2 * tq, :] = jnp.where(lane >= HEAD_DIM, q, zero)
    m_ref[...] = jnp.full(m_ref.shape, NEG_BIG, _F32)
    l_ref[...] = jnp.zeros(l_ref.shape, _F32)
    acc_ref[...] = jnp.zeros(acc_ref.shape, _F32)

    def step(start, masked):
        k = k_ref[pl.ds(start, tk), :]
        vt = vt_ref[:, pl.ds(start, tk)]
        s = lax.dot_general(k, qs_ref[...], (((1,), (1,)), ((), ())),
                            preferred_element_type=_F32)
        if masked:
            kv_pos = lax.broadcasted_iota(jnp.int32, (tk, tq), 0) + start
            q_pos = lax.broadcasted_iota(jnp.int32, (tk, tq), 1) + i * tq
            keep = kv_pos <= q_pos
            s = jnp.where(jnp.concatenate([keep, keep], axis=1), s, NEG_BIG)
        m_prev = m_ref[...]
        m_new = jnp.maximum(m_prev, jnp.max(s, axis=0, keepdims=True))
        alpha = jnp.exp2(m_prev - m_new)
        p = jnp.exp2(s - m_new)
        l_ref[...] = alpha * l_ref[...] + jnp.sum(p, axis=0, keepdims=True)
        acc_ref[...] = alpha * acc_ref[...] + jnp.dot(vt, p.astype(_BF16),
                                                      preferred_element_type=_F32)
        m_ref[...] = m_new

    n_full = (i * tq) // tk

    def body(j, carry):
        step(pl.multiple_of(j * tk, tk), False)
        return carry

    lax.fori_loop(0, n_full, body, 0)
    step(pl.multiple_of(n_full * tk, tk), True)

    lp = lp_ref[...]
    lam0 = lam0_ref[0:1, 0:1]
    lam = (jnp.exp(jnp.sum(lp[0:1, :] * lp[1:2, :], axis=-1, keepdims=True))
           - jnp.exp(jnp.sum(lp[2:3, :] * lp[3:4, :], axis=-1, keepdims=True)) + lam0)
    o = acc_ref[...] * (1.0 / l_ref[...])
    a = o[:, 0:tq] - lam * o[:, tq:2 * tq]
    a = a * lax.rsqrt(jnp.mean(a * a, axis=0, keepdims=True) + EPS)
    o_ref[...] = (a.T * gs_ref[...] * (1.0 - lam0)).astype(o_ref.dtype)


def _attn(lam0, lp, gs, q, k, vt):
    s = q.shape[0]
    tq = ATTN_TQ
    return pl.pallas_call(
        _attn_kernel,
        grid=(ATTN_HEADS, s // tq),
        in_specs=[pl.BlockSpec((1, 128), lambda h, i: (0, 0)),
                  pl.BlockSpec((4, HEAD_DIM), lambda h, i: (0, 0)),
                  pl.BlockSpec((1, V_DIM), lambda h, i: (0, 0)),
                  pl.BlockSpec((tq, V_DIM), lambda h, i: (i, h)),
                  pl.BlockSpec((s, V_DIM), lambda h, i: (0, h)),
                  pl.BlockSpec((V_DIM, s), lambda h, i: (h, 0))],
        out_specs=pl.BlockSpec((tq, V_DIM), lambda h, i: (i, h)),
        out_shape=jax.ShapeDtypeStruct((s, ATTN_WIDTH), _BF16),
        scratch_shapes=[pltpu.VMEM((2 * tq, V_DIM), _BF16),
                        pltpu.VMEM((1, 2 * tq), _F32),
                        pltpu.VMEM((1, 2 * tq), _F32),
                        pltpu.VMEM((V_DIM, 2 * tq), _F32)],
        compiler_params=pltpu.CompilerParams(dimension_semantics=("arbitrary", "arbitrary"),
                                             vmem_limit_bytes=VMEM_LIMIT),
        name="attn",
    )(lam0, lp, gs, q, k, vt)


def _mix_kernel(x_ref, a_ref, xp_ref, halo_ref, gate_ref, pw_ref, ps_ref,
                wua_ref, wup_ref, wo_ref, o_ref):
    tm = ROW_TILE
    i = pl.program_id(0)

    halo = jnp.where(i > 0, halo_ref[...], 0.0)
    ext = jnp.concatenate([halo, xp_ref[...]], axis=0)
    t1 = (lax.broadcasted_iota(jnp.int32, (tm, POOL_GROUP_DIM), 0) + (i * tm + 1)).astype(_F32)
    sums = ext
    pooled = []
    shift = 1
    for g, w in enumerate(POOL_WINDOWS):
        while shift < w:
            sums = sums + pltpu.roll(sums, shift, axis=0)
            shift *= 2
        lo = g * POOL_GROUP_DIM
        win = sums[POOL_HALO:, lo:lo + POOL_GROUP_DIM]
        cur = ext[POOL_HALO:, lo:lo + POOL_GROUP_DIM]
        pooled.append(win / jnp.minimum(t1, float(w)) - cur)
    mixed = [jnp.dot(pooled[g].astype(_BF16), pw_ref[g], preferred_element_type=_F32)
             for g in range(POOL_GROUPS)]
    p = (jnp.concatenate(mixed, axis=-1) * ps_ref[...]).astype(_BF16)

    y_attn = jnp.dot(a_ref[...], wua_ref[...], preferred_element_type=_F32)
    y_pool = jnp.dot(p, wup_ref[...], preferred_element_type=_F32)
    gates = gate_ref[...].astype(_F32)
    merged = gates[:, :D_MODEL] * y_attn + gates[:, D_MODEL:] * y_pool
    o_ref[...] = x_ref[...] + jnp.dot(merged.astype(_BF16), wo_ref[...], preferred_element_type=_F32)


def _mix(x, a, xp, gates, pool_w, pool_scale, w_up_attn, w_up_pool, w_o):
    s = x.shape[0]
    tm = ROW_TILE
    row = lambda w: pl.BlockSpec((tm, w), lambda i: (i, 0))
    halo_blocks = tm // POOL_HALO
    halo = pl.BlockSpec((POOL_HALO, POOL_WIDTH), lambda i: (jnp.maximum(i * halo_blocks - 1, 0), 0))
    return pl.pallas_call(
        _mix_kernel,
        grid=(s // tm,),
        in_specs=[row(D_MODEL), row(ATTN_WIDTH), row(POOL_WIDTH), halo, row(GATE_WIDTH),
                  _const_spec((POOL_GROUPS, POOL_GROUP_DIM, POOL_GROUP_DIM)), _const_spec((1, POOL_WIDTH)),
                  _const_spec((ATTN_WIDTH, D_MODEL)), _const_spec((POOL_WIDTH, D_MODEL)),
                  _const_spec((D_MODEL, D_MODEL))],
        out_specs=row(D_MODEL),
        out_shape=jax.ShapeDtypeStruct((s, D_MODEL), _F32),
        compiler_params=pltpu.CompilerParams(dimension_semantics=("arbitrary",),
                                             vmem_limit_bytes=VMEM_LIMIT),
        name="mix",
    )(x, a, xp, xp, gates, pool_w, pool_scale, w_up_attn, w_up_pool, w_o)


def _mlp_kernel(x_ref, g2_ref, w1_ref, w2_ref, o_ref):
    x = x_ref[...]
    h = (x * _rms_scale(x) * g2_ref[...]).astype(_BF16)
    acc = x
    for c in range(D_FF // FF_CHUNK):
        lo = c * FF_CHUNK
        u = jnp.dot(h, w1_ref[:, lo:lo + FF_CHUNK], preferred_element_type=_F32)
        u = jnp.square(jnp.maximum(u, 0.0)).astype(_BF16)
        acc = acc + jnp.dot(u, w2_ref[lo:lo + FF_CHUNK, :], preferred_element_type=_F32)
    o_ref[...] = acc


def _mlp(x, g2, w1, w2):
    s = x.shape[0]
    tm = ROW_TILE
    row = pl.BlockSpec((tm, D_MODEL), lambda i: (i, 0))
    return pl.pallas_call(
        _mlp_kernel,
        grid=(s // tm,),
        in_specs=[row, _const_spec((1, D_MODEL)), _const_spec((D_MODEL, D_FF)), _const_spec((D_FF, D_MODEL))],
        out_specs=row,
        out_shape=jax.ShapeDtypeStruct((s, D_MODEL), _F32),
        compiler_params=pltpu.CompilerParams(dimension_semantics=("arbitrary",),
                                             vmem_limit_bytes=VMEM_LIMIT),
        name="mlp",
    )(x, g2, w1, w2)


def _lambda_init(layer_idx):
    return 0.8 - 0.6 * math.exp(-0.3 * layer_idx)


def kernel(x, norm1_g, w_in, b_gate, q_norm_g, k_norm_g, lam_params, subln_g, pool_w, pool_scale,
           w_up_attn, w_up_pool, w_o, norm2_g, w_mlp_in, w_mlp_out):
    b, s, d = x.shape
    assert (b, d) == (1, D_MODEL) and s % ROW_TILE == 0 and s % ATTN_TK == 0
    xs = x.reshape(s, d)
    lane_group = jnp.arange(QK_WIDTH) // HEAD_DIM
    seg = jnp.where(lane_group[:, None] == lane_group[None, :], 1.0 / HEAD_DIM, 0.0).astype(_BF16)
    n_rep = QK_WIDTH // HEAD_DIM
    for l in range(DEPTH):
        gq = jnp.tile(q_norm_g[l], n_rep)[None, :] * (HEAD_DIM ** -0.5 * math.log2(math.e))
        gk = jnp.tile(k_norm_g[l], n_rep)[None, :]
        q, k, vt, xp, gates = _proj(xs, norm1_g[l][None, :], w_in[l].astype(_BF16), b_gate[l][None, :],
                                   gq, gk, seg)
        lam0 = jnp.full((1, 128), _lambda_init(l), _F32)
        a = _attn(lam0, lam_params[l], subln_g[l][None, :], q, k, vt)
        xs = _mix(xs, a, xp, gates, pool_w[l].astype(_BF16), pool_scale[l][None, :],
                  w_up_attn[l].astype(_BF16), w_up_pool[l].astype(_BF16), w_o[l].astype(_BF16))
        xs = _mlp(xs, norm2_g[l][None, :], w_mlp_in[l].astype(_BF16), w_mlp_out[l].astype(_BF16))
    return xs.reshape(b, s, d)
```

```python
import functools
import math

import jax
import jax.numpy as jnp
from jax import lax
from jax.experimental import pallas as pl
from jax.experimental.pallas import tpu as pltpu

D_MODEL = 1024
DEPTH = 4
ATTN_HEADS = 4
HEAD_DIM = 64
V_DIM = 2 * HEAD_DIM
ATTN_WIDTH = ATTN_HEADS * V_DIM
QK_WIDTH = ATTN_HEADS * 2 * HEAD_DIM
POOL_WINDOWS = (2, 4, 8, 16)
POOL_GROUPS = len(POOL_WINDOWS)
POOL_GROUP_DIM = 128
POOL_WIDTH = POOL_GROUPS * POOL_GROUP_DIM
GATE_WIDTH = 2 * D_MODEL
IN_COLS = 2 * QK_WIDTH + ATTN_WIDTH + POOL_WIDTH + GATE_WIDTH
D_FF = 4 * D_MODEL
EPS = 1e-6
NEG_BIG = -1e30

POOL_HALO = 16
ROW_TILE = 512
ATTN_TQ = 512
ATTN_TK = ATTN_TQ // 2
QK_ROWS = 128
MXU_COLS = 256
SM_ROWS = 32
SUM_ROWS = 16
ACC_ROWS = V_DIM + SUM_ROWS
PAIRS_PER_TRIP = 4
FF_CHUNK = 1024
VMEM_LIMIT = 56 * 1024 * 1024

_BF16 = jnp.bfloat16
_F32 = jnp.float32


def _const_spec(shape):
    return pl.BlockSpec(shape, lambda *_: (0,) * len(shape), pipeline_mode=pl.Buffered(1))


def _rms_scale(x):
    return lax.rsqrt(jnp.mean(x * x, axis=-1, keepdims=True) + EPS)


def _proj_kernel(x_ref, g1_ref, w_ref, bg_ref, gq_ref, gk_ref, seg_ref,
                 q_ref, k_ref, vt_ref, xp_ref, gate_ref):
    x = x_ref[...]
    h = (x * _rms_scale(x) * g1_ref[...]).astype(_BF16)

    def cols(lo, hi):
        return jnp.dot(h, w_ref[:, lo:hi], preferred_element_type=_F32)

    def qk_norm(t, g_ref):
        ms = jnp.dot((t * t).astype(_BF16), seg_ref[...], preferred_element_type=_F32)
        return (t * lax.rsqrt(ms + EPS) * g_ref[...]).astype(_BF16)

    c_q, c_k = QK_WIDTH, 2 * QK_WIDTH
    c_v, c_p = c_k + ATTN_WIDTH, c_k + ATTN_WIDTH + POOL_WIDTH
    q_ref[...] = qk_norm(cols(0, c_q), gq_ref)
    k_ref[...] = qk_norm(cols(c_q, c_k), gk_ref)
    v = cols(c_k, c_v)
    for hd in range(ATTN_HEADS):
        vt_ref[hd * V_DIM:(hd + 1) * V_DIM, :] = v[:, hd * V_DIM:(hd + 1) * V_DIM].T.astype(_BF16)
    xp_ref[...] = cols(c_v, c_p)
    gate_ref[...] = jax.nn.sigmoid(cols(c_p, IN_COLS) + bg_ref[...]).astype(_BF16)


def _proj(x, g1, w_in, b_gate, gq, gk, seg):
    s = x.shape[0]
    tm = ROW_TILE
    row = lambda w: pl.BlockSpec((tm, w), lambda i: (i, 0))
    return pl.pallas_call(
        _proj_kernel,
        grid=(s // tm,),
        in_specs=[row(D_MODEL), _const_spec((1, D_MODEL)), _const_spec((D_MODEL, IN_COLS)),
                  _const_spec((1, GATE_WIDTH)), _const_spec((1, QK_WIDTH)), _const_spec((1, QK_WIDTH)),
                  _const_spec((QK_WIDTH, QK_WIDTH))],
        out_specs=[row(QK_WIDTH), row(QK_WIDTH), pl.BlockSpec((ATTN_WIDTH, tm), lambda i: (0, i)),
                   row(POOL_WIDTH), row(GATE_WIDTH)],
        out_shape=[jax.ShapeDtypeStruct((s, QK_WIDTH), _BF16), jax.ShapeDtypeStruct((s, QK_WIDTH), _BF16),
                   jax.ShapeDtypeStruct((ATTN_WIDTH, s), _BF16), jax.ShapeDtypeStruct((s, POOL_WIDTH), _F32),
                   jax.ShapeDtypeStruct((s, GATE_WIDTH), _BF16)],
        compiler_params=pltpu.CompilerParams(dimension_semantics=("arbitrary",),
                                             vmem_limit_bytes=VMEM_LIMIT),
        name="proj",
    )(x, g1, w_in, b_gate, gq, gk, seg)


def _attn_kernel(lam0_ref, lp_ref, gs_ref, q_ref, k_ref, vt_ref, o_ref,
                 qt_ref, s0_ref, s1_ref, mb0_ref, mb1_ref, p0_ref, p1_ref, al0_ref, al1_ref,
                 m_ref, acc_ref):
    tq, tk = ATTN_TQ, ATTN_TK
    nq = 2 * tq
    i = pl.program_id(1)
    s_slots, mb_slots = (s0_ref, s1_ref), (mb0_ref, mb1_ref)
    p_slots, al_slots = (p0_ref, p1_ref), (al0_ref, al1_ref)
    n_ctiles = nq // MXU_COLS
    n_sm = QK_ROWS // SM_ROWS

    qf = q_ref[...].astype(_F32)
    lane = lax.broadcasted_iota(jnp.int32, qf.shape, 1)
    qt_ref[:, 0:tq] = jnp.where(lane < HEAD_DIM, qf, 0.0).T.astype(_BF16)
    qt_ref[:, tq:nq] = jnp.where(lane >= HEAD_DIM, qf, 0.0).T.astype(_BF16)
    m_ref[...] = jnp.full(m_ref.shape, NEG_BIG, _F32)
    acc_ref[...] = jnp.zeros(acc_ref.shape, _F32)
    p1_ref[...] = jnp.zeros(p1_ref.shape, _BF16)
    al1_ref[...] = jnp.ones(al1_ref.shape, _F32)

    def aligned(start, mult):
        return start if isinstance(start, int) else pl.multiple_of(start, mult)

    def col(c):
        return slice(c * MXU_COLS, (c + 1) * MXU_COLS)

    def qk_chunk(b, slot, g, c):
        row0 = g * QK_ROWS
        k = k_ref[pl.ds(aligned(b * tk + row0, QK_ROWS), QK_ROWS), :]
        s = jnp.dot(k, qt_ref[:, col(c)], preferred_element_type=_F32)
        s_slots[slot][row0:row0 + QK_ROWS, col(c)] = s
        cmax = jnp.max(s.reshape(QK_ROWS // 8, 8, MXU_COLS), axis=0)
        if g > 0:
            cmax = jnp.maximum(mb_slots[slot][:, col(c)], cmax)
        mb_slots[slot][:, col(c)] = cmax

    def qk_group(b, slot, g):
        return [functools.partial(qk_chunk, b, slot, g, c) for c in range(n_ctiles)]

    def pv_chunk(b, slot, c):
        vt = vt_ref[:, pl.ds(aligned(b * tk, tk), tk)]
        lhs = jnp.concatenate([vt, jnp.ones((SUM_ROWS, tk), _BF16)], axis=0)
        res = jnp.dot(lhs, p_slots[slot][:, col(c)], preferred_element_type=_F32)
        acc = acc_ref[:, col(c)].reshape(ACC_ROWS // 8, 8, MXU_COLS)
        acc = acc * al_slots[slot][:, col(c)][None] + res.reshape(ACC_ROWS // 8, 8, MXU_COLS)
        acc_ref[:, col(c)] = acc.reshape(ACC_ROWS, MXU_COLS)

    def pv_half(b, slot, half):
        per = n_ctiles // 2
        return [functools.partial(pv_chunk, b, slot, c) for c in range(half * per, (half + 1) * per)]

    def softmax_begin(slot, blk_max8):
        m_prev = m_ref[...]
        m_new = jnp.maximum(m_prev, jnp.max(blk_max8, axis=0, keepdims=True))
        alpha = jnp.exp2(m_prev - m_new)
        m_ref[...] = m_new
        al_slots[slot][...] = alpha

    def load_scores(slot, r, kv0):
        s = s_slots[slot][r:r + SM_ROWS, :]
        if kv0 is not None:
            kv_pos = lax.broadcasted_iota(jnp.int32, s.shape, 0) + (kv0 + r)
            q_pos = (lax.broadcasted_iota(jnp.int32, s.shape, 1) & (tq - 1)) + i * tq
            s = jnp.where(kv_pos <= q_pos, s, NEG_BIG)
        return s.reshape(SM_ROWS // 8, 8, nq)

    def softmax_chunk(slot, r, kv0=None):
        p = jnp.exp2(load_scores(slot, r, kv0) - m_ref[...][None])
        p_slots[slot][r:r + SM_ROWS, :] = p.reshape(SM_ROWS, nq).astype(_BF16)

    def softmax_group(slot, g, mxu_work, kv0=None):
        n_w = len(mxu_work)
        for c in range(n_sm):
            for w in mxu_work[-(-c * n_w // n_sm):-(-(c + 1) * n_w // n_sm)]:
                w()
            softmax_chunk(slot, g * QK_ROWS + c * SM_ROWS, kv0)

    def masked_block_max(slot, kv0):
        bm = None
        for r in range(0, tk, SM_ROWS):
            cm = jnp.max(load_scores(slot, r, kv0), axis=0)
            bm = cm if bm is None else jnp.maximum(bm, cm)
        return bm

    def pair_body(pair):
        b = 2 * pair
        pending = jnp.maximum(b - 3, 0)
        softmax_begin(0, mb0_ref[...])
        softmax_group(0, 0, qk_group(b - 1, 1, 1) + pv_half(pending, 1, 0))
        softmax_group(0, 1, qk_group(b, 0, 0) + pv_half(pending, 1, 1))
        softmax_begin(1, mb1_ref[...])
        softmax_group(1, 0, qk_group(b, 0, 1) + pv_half(b - 2, 0, 0))
        softmax_group(1, 1, qk_group(b + 1, 1, 0) + pv_half(b - 2, 0, 1))

    for w in qk_group(0, 0, 0) + qk_group(0, 0, 1) + qk_group(1, 1, 0):
        w()

    rem = i & (PAIRS_PER_TRIP - 1)
    bit = 1
    while bit < PAIRS_PER_TRIP:
        @pl.when((rem & bit) != 0)
        def _(bit=bit):
            for t in range(bit):
                pair_body(1 + (rem & (bit - 1)) + t)
        bit *= 2

    def body(j, carry):
        for t in range(PAIRS_PER_TRIP):
            pair_body(1 + rem + PAIRS_PER_TRIP * j + t)
        return carry

    lax.fori_loop(0, i // PAIRS_PER_TRIP, body, 0)

    b0 = 2 * i
    pending = jnp.maximum(b0 - 1, 0)
    for w in qk_group(b0 + 1, 1, 1) + pv_half(pending, 1, 0) + pv_half(pending, 1, 1):
        w()
    for slot in range(2):
        kv0 = (b0 + slot) * tk
        softmax_begin(slot, masked_block_max(slot, kv0))
        for g in range(tk // QK_ROWS):
            softmax_group(slot, g, [], kv0)
        for w in pv_half(b0 + slot, slot, 0) + pv_half(b0 + slot, slot, 1):
            w()

    lp = lp_ref[...]
    lam0 = lam0_ref[0:1, 0:1]
    lam = (jnp.exp(jnp.sum(lp[0:1, :] * lp[1:2, :], axis=-1, keepdims=True))
           - jnp.exp(jnp.sum(lp[2:3, :] * lp[3:4, :], axis=-1, keepdims=True)) + lam0)
    o = acc_ref[0:V_DIM, :] * (1.0 / acc_ref[V_DIM:V_DIM + 1, :])
    a = o[:, 0:tq] - lam * o[:, tq:nq]
    a = a * lax.rsqrt(jnp.mean(a * a, axis=0, keepdims=True) + EPS)
    o_ref[...] = (a.T * gs_ref[...] * (1.0 - lam0)).astype(o_ref.dtype)


def _attn(lam0, lp, gs, q, k, vt):
    s = q.shape[0]
    tq, tk = ATTN_TQ, ATTN_TK
    nq = 2 * tq
    return pl.pallas_call(
        _attn_kernel,
        grid=(ATTN_HEADS, s // tq),
        in_specs=[pl.BlockSpec((1, 128), lambda h, i: (0, 0)),
                  pl.BlockSpec((4, HEAD_DIM), lambda h, i: (0, 0)),
                  pl.BlockSpec((1, V_DIM), lambda h, i: (0, 0)),
                  pl.BlockSpec((tq, V_DIM), lambda h, i: (i, h)),
                  pl.BlockSpec((s, V_DIM), lambda h, i: (0, h)),
                  pl.BlockSpec((V_DIM, s), lambda h, i: (h, 0))],
        out_specs=pl.BlockSpec((tq, V_DIM), lambda h, i: (i, h)),
        out_shape=jax.ShapeDtypeStruct((s, ATTN_WIDTH), _BF16),
        scratch_shapes=[pltpu.VMEM((V_DIM, nq), _BF16),
                        pltpu.VMEM((tk, nq), _F32),
                        pltpu.VMEM((tk, nq), _F32),
                        pltpu.VMEM((8, nq), _F32),
                        pltpu.VMEM((8, nq), _F32),
                        pltpu.VMEM((tk, nq), _BF16),
                        pltpu.VMEM((tk, nq), _BF16),
                        pltpu.VMEM((8, nq), _F32),
                        pltpu.VMEM((8, nq), _F32),
                        pltpu.VMEM((8, nq), _F32),
                        pltpu.VMEM((ACC_ROWS, nq), _F32)],
        compiler_params=pltpu.CompilerParams(dimension_semantics=("arbitrary", "arbitrary"),
                                             vmem_limit_bytes=VMEM_LIMIT),
        name="attn",
    )(lam0, lp, gs, q, k, vt)


def _mix_kernel(x_ref, a_ref, xp_ref, halo_ref, gate_ref, pw_ref, ps_ref,
                wua_ref, wup_ref, wo_ref, o_ref):
    tm = ROW_TILE
    i = pl.program_id(0)

    halo = jnp.where(i > 0, halo_ref[...], 0.0)
    ext = jnp.concatenate([halo, xp_ref[...]], axis=0)
    t1 = (lax.broadcasted_iota(jnp.int32, (tm, POOL_GROUP_DIM), 0) + (i * tm + 1)).astype(_F32)
    sums = ext
    pooled = []
    shift = 1
    for g, w in enumerate(POOL_WINDOWS):
        while shift < w:
            sums = sums + pltpu.roll(sums, shift, axis=0)
            shift *= 2
        lo = g * POOL_GROUP_DIM
        win = sums[POOL_HALO:, lo:lo + POOL_GROUP_DIM]
        cur = ext[POOL_HALO:, lo:lo + POOL_GROUP_DIM]
        pooled.append(win / jnp.minimum(t1, float(w)) - cur)
    mixed = [jnp.dot(pooled[g].astype(_BF16), pw_ref[g], preferred_element_type=_F32)
             for g in range(POOL_GROUPS)]
    p = (jnp.concatenate(mixed, axis=-1) * ps_ref[...]).astype(_BF16)

    y_attn = jnp.dot(a_ref[...], wua_ref[...], preferred_element_type=_F32)
    y_pool = jnp.dot(p, wup_ref[...], preferred_element_type=_F32)
    gates = gate_ref[...].astype(_F32)
    merged = gates[:, :D_MODEL] * y_attn + gates[:, D_MODEL:] * y_pool
    o_ref[...] = x_ref[...] + jnp.dot(merged.astype(_BF16), wo_ref[...], preferred_element_type=_F32)


def _mix(x, a, xp, gates, pool_w, pool_scale, w_up_attn, w_up_pool, w_o):
    s = x.shape[0]
    tm = ROW_TILE
    row = lambda w: pl.BlockSpec((tm, w), lambda i: (i, 0))
    halo_blocks = tm // POOL_HALO
    halo = pl.BlockSpec((POOL_HALO, POOL_WIDTH), lambda i: (jnp.maximum(i * halo_blocks - 1, 0), 0))
    return pl.pallas_call(
        _mix_kernel,
        grid=(s // tm,),
        in_specs=[row(D_MODEL), row(ATTN_WIDTH), row(POOL_WIDTH), halo, row(GATE_WIDTH),
                  _const_spec((POOL_GROUPS, POOL_GROUP_DIM, POOL_GROUP_DIM)), _const_spec((1, POOL_WIDTH)),
                  _const_spec((ATTN_WIDTH, D_MODEL)), _const_spec((POOL_WIDTH, D_MODEL)),
                  _const_spec((D_MODEL, D_MODEL))],
        out_specs=row(D_MODEL),
        out_shape=jax.ShapeDtypeStruct((s, D_MODEL), _F32),
        compiler_params=pltpu.CompilerParams(dimension_semantics=("arbitrary",),
                                             vmem_limit_bytes=VMEM_LIMIT),
        name="mix",
    )(x, a, xp, xp, gates, pool_w, pool_scale, w_up_attn, w_up_pool, w_o)


def _mlp_kernel(x_ref, g2_ref, w1_ref, w2_ref, o_ref):
    x = x_ref[...]
    h = (x * _rms_scale(x) * g2_ref[...]).astype(_BF16)
    acc = x
    for c in range(D_FF // FF_CHUNK):
        lo = c * FF_CHUNK
        u = jnp.dot(h, w1_ref[:, lo:lo + FF_CHUNK], preferred_element_type=_F32)
        u = jnp.square(jnp.maximum(u, 0.0)).astype(_BF16)
        acc = acc + jnp.dot(u, w2_ref[lo:lo + FF_CHUNK, :], preferred_element_type=_F32)
    o_ref[...] = acc


def _mlp(x, g2, w1, w2):
    s = x.shape[0]
    tm = ROW_TILE
    row = pl.BlockSpec((tm, D_MODEL), lambda i: (i, 0))
    return pl.pallas_call(
        _mlp_kernel,
        grid=(s // tm,),
        in_specs=[row, _const_spec((1, D_MODEL)), _const_spec((D_MODEL, D_FF)), _const_spec((D_FF, D_MODEL))],
        out_specs=row,
        out_shape=jax.ShapeDtypeStruct((s, D_MODEL), _F32),
        compiler_params=pltpu.CompilerParams(dimension_semantics=("arbitrary",),
                                             vmem_limit_bytes=VMEM_LIMIT),
        name="mlp",
    )(x, g2, w1, w2)


def _lambda_init(layer_idx):
    return 0.8 - 0.6 * math.exp(-0.3 * layer_idx)


def kernel(x, norm1_g, w_in, b_gate, q_norm_g, k_norm_g, lam_params, subln_g, pool_w, pool_scale,
           w_up_attn, w_up_pool, w_o, norm2_g, w_mlp_in, w_mlp_out):
    b, s, d = x.shape
    assert (b, d) == (1, D_MODEL) and s % ROW_TILE == 0 and s % ATTN_TQ == 0
    xs = x.reshape(s, d)
    lane_group = jnp.arange(QK_WIDTH) // HEAD_DIM
    seg = jnp.where(lane_group[:, None] == lane_group[None, :], 1.0 / HEAD_DIM, 0.0).astype(_BF16)
    n_rep = QK_WIDTH // HEAD_DIM
    for l in range(DEPTH):
        gq = jnp.tile(q_norm_g[l], n_rep)[None, :] * (HEAD_DIM ** -0.5 * math.log2(math.e))
        gk = jnp.tile(k_norm_g[l], n_rep)[None, :]
        q, k, vt, xp, gates = _proj(xs, norm1_g[l][None, :], w_in[l].astype(_BF16), b_gate[l][None, :],
                                   gq, gk, seg)
        lam0 = jnp.full((1, 128), _lambda_init(l), _F32)
        a = _attn(lam0, lam_params[l], subln_g[l][None, :], q, k, vt)
        xs = _mix(xs, a, xp, gates, pool_w[l].astype(_BF16), pool_scale[l][None, :],
                  w_up_attn[l].astype(_BF16), w_up_pool[l].astype(_BF16), w_o[l].astype(_BF16))
        xs = _mlp(xs, norm2_g[l][None, :], w_mlp_in[l].astype(_BF16), w_mlp_out[l].astype(_BF16))
    return xs.reshape(b, s, d)
```

```python
import functools
import math

import jax
import jax.numpy as jnp
from jax import lax
from jax.experimental import pallas as pl
from jax.experimental.pallas import tpu as pltpu

D_MODEL = 1024
DEPTH = 4
ATTN_HEADS = 4
HEAD_DIM = 64
V_DIM = 2 * HEAD_DIM
ATTN_WIDTH = ATTN_HEADS * V_DIM
QK_WIDTH = ATTN_HEADS * 2 * HEAD_DIM
POOL_WINDOWS = (2, 4, 8, 16)
POOL_GROUPS = len(POOL_WINDOWS)
POOL_GROUP_DIM = 128
POOL_WIDTH = POOL_GROUPS * POOL_GROUP_DIM
GATE_WIDTH = 2 * D_MODEL
IN_COLS = 2 * QK_WIDTH + ATTN_WIDTH + POOL_WIDTH + GATE_WIDTH
D_FF = 4 * D_MODEL
EPS = 1e-6
NEG_BIG = -1e30

POOL_HALO = 16
ROW_TILE = 512
ATTN_TQ = 512
ATTN_TK = ATTN_TQ // 2
QK_ROWS = 128
MXU_COLS = 256
SM_ROWS = 32
SUM_ROWS = 16
ACC_ROWS = V_DIM + SUM_ROWS
PAIRS_PER_TRIP = 8
FF_CHUNK = 1024
VMEM_LIMIT = 56 * 1024 * 1024

_BF16 = jnp.bfloat16
_F32 = jnp.float32


def _const_spec(shape):
    return pl.BlockSpec(shape, lambda *_: (0,) * len(shape), pipeline_mode=pl.Buffered(1))


def _rms_scale(x):
    return lax.rsqrt(jnp.mean(x * x, axis=-1, keepdims=True) + EPS)


def _proj_kernel(x_ref, g1_ref, w_ref, bg_ref, gq_ref, gk_ref, seg_ref,
                 qt_ref, k_ref, vt_ref, xp_ref, gate_ref):
    x = x_ref[...]
    h = (x * _rms_scale(x) * g1_ref[...]).astype(_BF16)

    def cols(lo, hi):
        return jnp.dot(h, w_ref[:, lo:hi], preferred_element_type=_F32)

    def qk_norm(t, g_ref):
        ms = jnp.dot((t * t).astype(_BF16), seg_ref[...], preferred_element_type=_F32)
        return t * lax.rsqrt(ms + EPS) * g_ref[...]

    def store_head_major(dst_ref, t):
        for hd in range(ATTN_HEADS):
            dst_ref[hd * V_DIM:(hd + 1) * V_DIM, :] = t[:, hd * V_DIM:(hd + 1) * V_DIM].T.astype(_BF16)

    c_q, c_k = QK_WIDTH, 2 * QK_WIDTH
    c_v, c_p = c_k + ATTN_WIDTH, c_k + ATTN_WIDTH + POOL_WIDTH
    store_head_major(qt_ref, qk_norm(cols(0, c_q), gq_ref))
    k_ref[...] = qk_norm(cols(c_q, c_k), gk_ref).astype(_BF16)
    store_head_major(vt_ref, cols(c_k, c_v))
    xp_ref[...] = cols(c_v, c_p)
    gate_ref[...] = jax.nn.sigmoid(cols(c_p, IN_COLS) + bg_ref[...]).astype(_BF16)


def _proj(x, g1, w_in, b_gate, gq, gk, seg):
    s = x.shape[0]
    tm = ROW_TILE
    row = lambda w: pl.BlockSpec((tm, w), lambda i: (i, 0))
    return pl.pallas_call(
        _proj_kernel,
        grid=(s // tm,),
        in_specs=[row(D_MODEL), _const_spec((1, D_MODEL)), _const_spec((D_MODEL, IN_COLS)),
                  _const_spec((1, GATE_WIDTH)), _const_spec((1, QK_WIDTH)), _const_spec((1, QK_WIDTH)),
                  _const_spec((QK_WIDTH, QK_WIDTH))],
        out_specs=[pl.BlockSpec((QK_WIDTH, tm), lambda i: (0, i)), row(QK_WIDTH),
                   pl.BlockSpec((ATTN_WIDTH, tm), lambda i: (0, i)), row(POOL_WIDTH), row(GATE_WIDTH)],
        out_shape=[jax.ShapeDtypeStruct((QK_WIDTH, s), _BF16), jax.ShapeDtypeStruct((s, QK_WIDTH), _BF16),
                   jax.ShapeDtypeStruct((ATTN_WIDTH, s), _BF16), jax.ShapeDtypeStruct((s, POOL_WIDTH), _F32),
                   jax.ShapeDtypeStruct((s, GATE_WIDTH), _BF16)],
        compiler_params=pltpu.CompilerParams(dimension_semantics=("arbitrary",),
                                             vmem_limit_bytes=VMEM_LIMIT),
        name="proj",
    )(x, g1, w_in, b_gate, gq, gk, seg)


def _attn_kernel(lam0_ref, lp_ref, gs_ref, q_ref, k_ref, vt_ref, o_ref,
                 qt_ref, s0_ref, s1_ref, mb0_ref, mb1_ref, p0_ref, p1_ref, al0_ref, al1_ref,
                 m_ref, acc_ref):
    tq, tk = ATTN_TQ, ATTN_TK
    nq = 2 * tq
    i = pl.program_id(1)
    s_slots, mb_slots = (s0_ref, s1_ref), (mb0_ref, mb1_ref)
    p_slots, al_slots = (p0_ref, p1_ref), (al0_ref, al1_ref)
    n_ctiles = nq // MXU_COLS
    n_sm = QK_ROWS // SM_ROWS

    _store_branch_qt(q_ref, qt_ref)
    m_ref[...] = jnp.full(m_ref.shape, NEG_BIG, _F32)
    acc_ref[...] = jnp.zeros(acc_ref.shape, _F32)
    p1_ref[...] = jnp.zeros(p1_ref.shape, _BF16)
    al1_ref[...] = jnp.ones(al1_ref.shape, _F32)

    def aligned(start, mult):
        return start if isinstance(start, int) else pl.multiple_of(start, mult)

    def col(c):
        return slice(c * MXU_COLS, (c + 1) * MXU_COLS)

    def qk_chunk(b, slot, g, c):
        row0 = g * QK_ROWS
        k = k_ref[pl.ds(aligned(b * tk + row0, QK_ROWS), QK_ROWS), :]
        s = jnp.dot(k, qt_ref[:, col(c)], preferred_element_type=_F32)
        s_slots[slot][row0:row0 + QK_ROWS, col(c)] = s
        cmax = jnp.max(s.reshape(QK_ROWS // 8, 8, MXU_COLS), axis=0)
        if g > 0:
            cmax = jnp.maximum(mb_slots[slot][:, col(c)], cmax)
        mb_slots[slot][:, col(c)] = cmax

    def qk_group(b, slot, g):
        return [functools.partial(qk_chunk, b, slot, g, c) for c in range(n_ctiles)]

    def pv_chunk(b, slot, c):
        vt = vt_ref[:, pl.ds(aligned(b * tk, tk), tk)]
        lhs = jnp.concatenate([vt, jnp.ones((SUM_ROWS, tk), _BF16)], axis=0)
        res = jnp.dot(lhs, p_slots[slot][:, col(c)], preferred_element_type=_F32)
        acc = acc_ref[:, col(c)].reshape(ACC_ROWS // 8, 8, MXU_COLS)
        acc = acc * al_slots[slot][:, col(c)][None] + res.reshape(ACC_ROWS // 8, 8, MXU_COLS)
        acc_ref[:, col(c)] = acc.reshape(ACC_ROWS, MXU_COLS)

    def pv_half(b, slot, half):
        per = n_ctiles // 2
        return [functools.partial(pv_chunk, b, slot, c) for c in range(half * per, (half + 1) * per)]

    def softmax_begin(slot, blk_max8):
        m_prev = m_ref[...]
        m_new = jnp.maximum(m_prev, jnp.max(blk_max8, axis=0, keepdims=True))
        alpha = jnp.exp2(m_prev - m_new)
        m_ref[...] = m_new
        al_slots[slot][...] = alpha

    def softmax_chunk(slot, r):
        s = s_slots[slot][r:r + SM_ROWS, :].reshape(SM_ROWS // 8, 8, nq)
        p = jnp.exp2(s - m_ref[...][None])
        p_slots[slot][r:r + SM_ROWS, :] = p.reshape(SM_ROWS, nq).astype(_BF16)

    def softmax_group(slot, g, mxu_work):
        n_w = len(mxu_work)
        for c in range(n_sm):
            for w in mxu_work[-(-c * n_w // n_sm):-(-(c + 1) * n_w // n_sm)]:
                w()
            softmax_chunk(slot, g * QK_ROWS + c * SM_ROWS)

    def mask_block(slot, kv0):
        bm = None
        for r in range(0, tk, SM_ROWS):
            s = s_slots[slot][r:r + SM_ROWS, :]
            kv_pos = lax.broadcasted_iota(jnp.int32, s.shape, 0) + (kv0 + r)
            q_pos = (lax.broadcasted_iota(jnp.int32, s.shape, 1) & (tq - 1)) + i * tq
            s = jnp.where(kv_pos <= q_pos, s, NEG_BIG)
            s_slots[slot][r:r + SM_ROWS, :] = s
            cm = jnp.max(s.reshape(SM_ROWS // 8, 8, nq), axis=0)
            bm = cm if bm is None else jnp.maximum(bm, cm)
        return bm

    def pair_body(pair):
        b = 2 * pair
        pending = jnp.maximum(b - 3, 0)
        softmax_begin(0, mb0_ref[...])
        softmax_group(0, 0, qk_group(b - 1, 1, 1) + pv_half(pending, 1, 0))
        softmax_group(0, 1, qk_group(b, 0, 0) + pv_half(pending, 1, 1))
        softmax_begin(1, mb1_ref[...])
        softmax_group(1, 0, qk_group(b, 0, 1) + pv_half(b - 2, 0, 0))
        softmax_group(1, 1, qk_group(b + 1, 1, 0) + pv_half(b - 2, 0, 1))

    for w in qk_group(0, 0, 0) + qk_group(0, 0, 1) + qk_group(1, 1, 0):
        w()

    rem = i & (PAIRS_PER_TRIP - 1)
    bit = 1
    while bit < PAIRS_PER_TRIP:
        @pl.when((rem & bit) != 0)
        def _(bit=bit):
            for t in range(bit):
                pair_body(1 + (rem & (bit - 1)) + t)
        bit *= 2

    def body(j, carry):
        for t in range(PAIRS_PER_TRIP):
            pair_body(1 + rem + PAIRS_PER_TRIP * j + t)
        return carry

    lax.fori_loop(0, i // PAIRS_PER_TRIP, body, 0)

    b0 = 2 * i
    pending = jnp.maximum(b0 - 1, 0)
    for w in qk_group(b0 + 1, 1, 1) + pv_half(pending, 1, 0) + pv_half(pending, 1, 1):
        w()
    for slot in range(2):
        kv0 = (b0 + slot) * tk
        softmax_begin(slot, mask_block(slot, kv0))
        for g in range(tk // QK_ROWS):
            softmax_group(slot, g, [])
        for w in pv_half(b0 + slot, slot, 0) + pv_half(b0 + slot, slot, 1):
            w()

    _attn_finalize(lam0_ref, lp_ref, gs_ref, acc_ref, o_ref)


def _store_branch_qt(q_ref, qt_ref):
    tq = ATTN_TQ
    zeros = jnp.zeros((HEAD_DIM, tq), _BF16)
    qt_ref[0:HEAD_DIM, 0:tq] = q_ref[0:HEAD_DIM, :]
    qt_ref[HEAD_DIM:V_DIM, 0:tq] = zeros
    qt_ref[0:HEAD_DIM, tq:2 * tq] = zeros
    qt_ref[HEAD_DIM:V_DIM, tq:2 * tq] = q_ref[HEAD_DIM:V_DIM, :]


def _attn_finalize(lam0_ref, lp_ref, gs_ref, acc_ref, o_ref):
    tq = ATTN_TQ
    lp = lp_ref[...]
    lam0 = lam0_ref[0:1, 0:1]
    lam = (jnp.exp(jnp.sum(lp[0:1, :] * lp[1:2, :], axis=-1, keepdims=True))
           - jnp.exp(jnp.sum(lp[2:3, :] * lp[3:4, :], axis=-1, keepdims=True)) + lam0)
    o = acc_ref[0:V_DIM, :] * (1.0 / acc_ref[V_DIM:V_DIM + 1, :])
    a = o[:, 0:tq] - lam * o[:, tq:2 * tq]
    a = a * lax.rsqrt(jnp.mean(a * a, axis=0, keepdims=True) + EPS)
    o_ref[...] = (a.T * gs_ref[...] * (1.0 - lam0)).astype(o_ref.dtype)


def _attn(lam0, lp, gs, qt, k, vt):
    s = k.shape[0]
    tq, tk = ATTN_TQ, ATTN_TK
    nq = 2 * tq
    return pl.pallas_call(
        _attn_kernel,
        grid=(ATTN_HEADS, s // tq),
        in_specs=[pl.BlockSpec((1, 128), lambda h, i: (0, 0)),
                  pl.BlockSpec((4, HEAD_DIM), lambda h, i: (0, 0)),
                  pl.BlockSpec((1, V_DIM), lambda h, i: (0, 0)),
                  pl.BlockSpec((V_DIM, tq), lambda h, i: (h, i)),
                  pl.BlockSpec((s, V_DIM), lambda h, i: (0, h)),
                  pl.BlockSpec((V_DIM, s), lambda h, i: (h, 0))],
        out_specs=pl.BlockSpec((tq, V_DIM), lambda h, i: (i, h)),
        out_shape=jax.ShapeDtypeStruct((s, ATTN_WIDTH), _BF16),
        scratch_shapes=[pltpu.VMEM((V_DIM, nq), _BF16),
                        pltpu.VMEM((tk, nq), _F32),
                        pltpu.VMEM((tk, nq), _F32),
                        pltpu.VMEM((8, nq), _F32),
                        pltpu.VMEM((8, nq), _F32),
                        pltpu.VMEM((tk, nq), _BF16),
                        pltpu.VMEM((tk, nq), _BF16),
                        pltpu.VMEM((8, nq), _F32),
                        pltpu.VMEM((8, nq), _F32),
                        pltpu.VMEM((8, nq), _F32),
                        pltpu.VMEM((ACC_ROWS, nq), _F32)],
        compiler_params=pltpu.CompilerParams(dimension_semantics=("arbitrary", "arbitrary"),
                                             vmem_limit_bytes=VMEM_LIMIT),
        name="attn",
    )(lam0, lp, gs, qt, k, vt)


def _mix_kernel(x_ref, a_ref, xp_ref, halo_ref, gate_ref, pw_ref, ps_ref,
                wua_ref, wup_ref, wo_ref, o_ref):
    tm = ROW_TILE
    i = pl.program_id(0)

    halo = jnp.where(i > 0, halo_ref[...], 0.0)
    ext = jnp.concatenate([halo, xp_ref[...]], axis=0)
    t1 = (lax.broadcasted_iota(jnp.int32, (tm, POOL_GROUP_DIM), 0) + (i * tm + 1)).astype(_F32)
    sums = ext
    pooled = []
    shift = 1
    for g, w in enumerate(POOL_WINDOWS):
        while shift < w:
            sums = sums + pltpu.roll(sums, shift, axis=0)
            shift *= 2
        lo = g * POOL_GROUP_DIM
        win = sums[POOL_HALO:, lo:lo + POOL_GROUP_DIM]
        cur = ext[POOL_HALO:, lo:lo + POOL_GROUP_DIM]
        pooled.append(win / jnp.minimum(t1, float(w)) - cur)
    mixed = [jnp.dot(pooled[g].astype(_BF16), pw_ref[g], preferred_element_type=_F32)
             for g in range(POOL_GROUPS)]
    p = (jnp.concatenate(mixed, axis=-1) * ps_ref[...]).astype(_BF16)

    y_attn = jnp.dot(a_ref[...], wua_ref[...], preferred_element_type=_F32)
    y_pool = jnp.dot(p, wup_ref[...], preferred_element_type=_F32)
    gates = gate_ref[...].astype(_F32)
    merged = gates[:, :D_MODEL] * y_attn + gates[:, D_MODEL:] * y_pool
    o_ref[...] = x_ref[...] + jnp.dot(merged.astype(_BF16), wo_ref[...], preferred_element_type=_F32)


def _mix(x, a, xp, gates, pool_w, pool_scale, w_up_attn, w_up_pool, w_o):
    s = x.shape[0]
    tm = ROW_TILE
    row = lambda w: pl.BlockSpec((tm, w), lambda i: (i, 0))
    halo_blocks = tm // POOL_HALO
    halo = pl.BlockSpec((POOL_HALO, POOL_WIDTH), lambda i: (jnp.maximum(i * halo_blocks - 1, 0), 0))
    return pl.pallas_call(
        _mix_kernel,
        grid=(s // tm,),
        in_specs=[row(D_MODEL), row(ATTN_WIDTH), row(POOL_WIDTH), halo, row(GATE_WIDTH),
                  _const_spec((POOL_GROUPS, POOL_GROUP_DIM, POOL_GROUP_DIM)), _const_spec((1, POOL_WIDTH)),
                  _const_spec((ATTN_WIDTH, D_MODEL)), _const_spec((POOL_WIDTH, D_MODEL)),
                  _const_spec((D_MODEL, D_MODEL))],
        out_specs=row(D_MODEL),
        out_shape=jax.ShapeDtypeStruct((s, D_MODEL), _F32),
        compiler_params=pltpu.CompilerParams(dimension_semantics=("arbitrary",),
                                             vmem_limit_bytes=VMEM_LIMIT),
        name="mix",
    )(x, a, xp, xp, gates, pool_w, pool_scale, w_up_attn, w_up_pool, w_o)


def _mlp_kernel(x_ref, g2_ref, w1_ref, w2_ref, o_ref):
    x = x_ref[...]
    h = (x * _rms_scale(x) * g2_ref[...]).astype(_BF16)
    acc = x
    for c in range(D_FF // FF_CHUNK):
        lo = c * FF_CHUNK
        u = jnp.dot(h, w1_ref[:, lo:lo + FF_CHUNK], preferred_element_type=_F32)
        u = jnp.square(jnp.maximum(u, 0.0)).astype(_BF16)
        acc = acc + jnp.dot(u, w2_ref[lo:lo + FF_CHUNK, :], preferred_element_type=_F32)
    o_ref[...] = acc


def _mlp(x, g2, w1, w2):
    s = x.shape[0]
    tm = ROW_TILE
    row = pl.BlockSpec((tm, D_MODEL), lambda i: (i, 0))
    return pl.pallas_call(
        _mlp_kernel,
        grid=(s // tm,),
        in_specs=[row, _const_spec((1, D_MODEL)), _const_spec((D_MODEL, D_FF)), _const_spec((D_FF, D_MODEL))],
        out_specs=row,
        out_shape=jax.ShapeDtypeStruct((s, D_MODEL), _F32),
        compiler_params=pltpu.CompilerParams(dimension_semantics=("arbitrary",),
                                             vmem_limit_bytes=VMEM_LIMIT),
        name="mlp",
    )(x, g2, w1, w2)


def _lambda_init(layer_idx):
    return 0.8 - 0.6 * math.exp(-0.3 * layer_idx)


def kernel(x, norm1_g, w_in, b_gate, q_norm_g, k_norm_g, lam_params, subln_g, pool_w, pool_scale,
           w_up_attn, w_up_pool, w_o, norm2_g, w_mlp_in, w_mlp_out):
    b, s, d = x.shape
    assert (b, d) == (1, D_MODEL) and s % ROW_TILE == 0 and s % ATTN_TQ == 0
    xs = x.reshape(s, d)
    lane_group = jnp.arange(QK_WIDTH) // HEAD_DIM
    seg = jnp.where(lane_group[:, None] == lane_group[None, :], 1.0 / HEAD_DIM, 0.0).astype(_BF16)
    n_rep = QK_WIDTH // HEAD_DIM
    for l in range(DEPTH):
        gq = jnp.tile(q_norm_g[l], n_rep)[None, :] * (HEAD_DIM ** -0.5 * math.log2(math.e))
        gk = jnp.tile(k_norm_g[l], n_rep)[None, :]
        qt, k, vt, xp, gates = _proj(xs, norm1_g[l][None, :], w_in[l].astype(_BF16), b_gate[l][None, :],
                                   gq, gk, seg)
        lam0 = jnp.full((1, 128), _lambda_init(l), _F32)
        a = _attn(lam0, lam_params[l], subln_g[l][None, :], qt, k, vt)
        xs = _mix(xs, a, xp, gates, pool_w[l].astype(_BF16), pool_scale[l][None, :],
                  w_up_attn[l].astype(_BF16), w_up_pool[l].astype(_BF16), w_o[l].astype(_BF16))
        xs = _mlp(xs, norm2_g[l][None, :], w_mlp_in[l].astype(_BF16), w_mlp_out[l].astype(_BF16))
    return xs.reshape(b, s, d)
```

```python
import functools
import math

import jax
import jax.numpy as jnp
from jax import lax
from jax.experimental import pallas as pl
from jax.experimental.pallas import tpu as pltpu

D_MODEL = 1024
DEPTH = 4
ATTN_HEADS = 4
HEAD_DIM = 64
V_DIM = 2 * HEAD_DIM
ATTN_WIDTH = ATTN_HEADS * V_DIM
QK_WIDTH = ATTN_HEADS * 2 * HEAD_DIM
POOL_WINDOWS = (2, 4, 8, 16)
POOL_GROUPS = len(POOL_WINDOWS)
POOL_GROUP_DIM = 128
POOL_WIDTH = POOL_GROUPS * POOL_GROUP_DIM
GATE_WIDTH = 2 * D_MODEL
IN_COLS = 2 * QK_WIDTH + ATTN_WIDTH + POOL_WIDTH + GATE_WIDTH
D_FF = 4 * D_MODEL
EPS = 1e-6
NEG_BIG = -1e30

POOL_HALO = 16
ROW_TILE = 1024
ATTN_TQ = 512
ATTN_TK = ATTN_TQ // 2
QK_ROWS = 128
MXU_COLS = 256
SM_ROWS = 32
SUM_ROWS = 16
ACC_ROWS = V_DIM + SUM_ROWS
PAIRS_PER_TRIP = 8
FF_CHUNK = 1024
VMEM_LIMIT = 56 * 1024 * 1024

_BF16 = jnp.bfloat16
_F32 = jnp.float32


def _const_spec(shape):
    return pl.BlockSpec(shape, lambda *_: (0,) * len(shape), pipeline_mode=pl.Buffered(1))


def _rms_scale(x):
    return lax.rsqrt(jnp.mean(x * x, axis=-1, keepdims=True) + EPS)


def _proj_kernel(x_ref, g1_ref, w_ref, bg_ref, gq_ref, gk_ref, seg_ref,
                 qt_ref, k_ref, vt_ref, xp_ref, gate_ref):
    x = x_ref[...]
    h = (x * _rms_scale(x) * g1_ref[...]).astype(_BF16)

    def cols(lo, hi):
        return jnp.dot(h, w_ref[:, lo:hi], preferred_element_type=_F32)

    def qk_norm(t, g_ref):
        ms = jnp.dot((t * t).astype(_BF16), seg_ref[...], preferred_element_type=_F32)
        return t * lax.rsqrt(ms + EPS) * g_ref[...]

    def store_head_major(dst_ref, t):
        for hd in range(ATTN_HEADS):
            dst_ref[hd * V_DIM:(hd + 1) * V_DIM, :] = t[:, hd * V_DIM:(hd + 1) * V_DIM].T.astype(_BF16)

    c_q, c_k = QK_WIDTH, 2 * QK_WIDTH
    c_v, c_p = c_k + ATTN_WIDTH, c_k + ATTN_WIDTH + POOL_WIDTH
    store_head_major(qt_ref, qk_norm(cols(0, c_q), gq_ref))
    k_ref[...] = qk_norm(cols(c_q, c_k), gk_ref).astype(_BF16)
    store_head_major(vt_ref, cols(c_k, c_v))
    xp_ref[...] = cols(c_v, c_p)
    gate_ref[...] = jax.nn.sigmoid(cols(c_p, IN_COLS) + bg_ref[...]).astype(_BF16)


def _proj(x, g1, w_in, b_gate, gq, gk, seg):
    s = x.shape[0]
    tm = ROW_TILE
    row = lambda w: pl.BlockSpec((tm, w), lambda i: (i, 0))
    return pl.pallas_call(
        _proj_kernel,
        grid=(s // tm,),
        in_specs=[row(D_MODEL), _const_spec((1, D_MODEL)), _const_spec((D_MODEL, IN_COLS)),
                  _const_spec((1, GATE_WIDTH)), _const_spec((1, QK_WIDTH)), _const_spec((1, QK_WIDTH)),
                  _const_spec((QK_WIDTH, QK_WIDTH))],
        out_specs=[pl.BlockSpec((QK_WIDTH, tm), lambda i: (0, i)), row(QK_WIDTH),
                   pl.BlockSpec((ATTN_WIDTH, tm), lambda i: (0, i)), row(POOL_WIDTH), row(GATE_WIDTH)],
        out_shape=[jax.ShapeDtypeStruct((QK_WIDTH, s), _BF16), jax.ShapeDtypeStruct((s, QK_WIDTH), _BF16),
                   jax.ShapeDtypeStruct((ATTN_WIDTH, s), _BF16), jax.ShapeDtypeStruct((s, POOL_WIDTH), _F32),
                   jax.ShapeDtypeStruct((s, GATE_WIDTH), _BF16)],
        compiler_params=pltpu.CompilerParams(dimension_semantics=("arbitrary",),
                                             vmem_limit_bytes=VMEM_LIMIT),
        name="proj",
    )(x, g1, w_in, b_gate, gq, gk, seg)


def _attn_kernel(lam0_ref, lp_ref, gs_ref, q_ref, k_ref, vt_ref, o_ref,
                 qt_ref, s0_ref, s1_ref, mb0_ref, mb1_ref, p0_ref, p1_ref, al0_ref, al1_ref,
                 m_ref, acc_ref):
    tq, tk = ATTN_TQ, ATTN_TK
    nq = 2 * tq
    i = pl.program_id(1)
    s_slots, mb_slots = (s0_ref, s1_ref), (mb0_ref, mb1_ref)
    p_slots, al_slots = (p0_ref, p1_ref), (al0_ref, al1_ref)
    n_ctiles = nq // MXU_COLS
    n_sm = QK_ROWS // SM_ROWS

    _store_branch_qt(q_ref, qt_ref)
    m_ref[...] = jnp.full(m_ref.shape, NEG_BIG, _F32)
    acc_ref[...] = jnp.zeros(acc_ref.shape, _F32)
    p1_ref[...] = jnp.zeros(p1_ref.shape, _BF16)
    al1_ref[...] = jnp.ones(al1_ref.shape, _F32)

    def aligned(start, mult):
        return start if isinstance(start, int) else pl.multiple_of(start, mult)

    def col(c):
        return slice(c * MXU_COLS, (c + 1) * MXU_COLS)

    def qk_chunk(b, slot, g, c):
        row0 = g * QK_ROWS
        k = k_ref[pl.ds(aligned(b * tk + row0, QK_ROWS), QK_ROWS), :]
        s = jnp.dot(k, qt_ref[:, col(c)], preferred_element_type=_F32)
        s_slots[slot][row0:row0 + QK_ROWS, col(c)] = s
        cmax = jnp.max(s.reshape(QK_ROWS // 8, 8, MXU_COLS), axis=0)
        if g > 0:
            cmax = jnp.maximum(mb_slots[slot][:, col(c)], cmax)
        mb_slots[slot][:, col(c)] = cmax

    def qk_group(b, slot, g):
        return [functools.partial(qk_chunk, b, slot, g, c) for c in range(n_ctiles)]

    def pv_chunk(b, slot, c):
        vt = vt_ref[:, pl.ds(aligned(b * tk, tk), tk)]
        lhs = jnp.concatenate([vt, jnp.ones((SUM_ROWS, tk), _BF16)], axis=0)
        res = jnp.dot(lhs, p_slots[slot][:, col(c)], preferred_element_type=_F32)
        acc = acc_ref[:, col(c)].reshape(ACC_ROWS // 8, 8, MXU_COLS)
        acc = acc * al_slots[slot][:, col(c)][None] + res.reshape(ACC_ROWS // 8, 8, MXU_COLS)
        acc_ref[:, col(c)] = acc.reshape(ACC_ROWS, MXU_COLS)

    def pv_half(b, slot, half):
        per = n_ctiles // 2
        return [functools.partial(pv_chunk, b, slot, c) for c in range(half * per, (half + 1) * per)]

    def softmax_begin(slot, blk_max8):
        m_prev = m_ref[...]
        m_new = jnp.maximum(m_prev, jnp.max(blk_max8, axis=0, keepdims=True))
        alpha = jnp.exp2(m_prev - m_new)
        m_ref[...] = m_new
        al_slots[slot][...] = alpha

    def softmax_chunk(slot, r):
        s = s_slots[slot][r:r + SM_ROWS, :].reshape(SM_ROWS // 8, 8, nq)
        p = jnp.exp2(s - m_ref[...][None])
        p_slots[slot][r:r + SM_ROWS, :] = p.reshape(SM_ROWS, nq).astype(_BF16)

    def softmax_group(slot, g, mxu_work):
        n_w = len(mxu_work)
        for c in range(n_sm):
            for w in mxu_work[-(-c * n_w // n_sm):-(-(c + 1) * n_w // n_sm)]:
                w()
            softmax_chunk(slot, g * QK_ROWS + c * SM_ROWS)

    def mask_block(slot, kv0):
        bm = None
        for r in range(0, tk, SM_ROWS):
            s = s_slots[slot][r:r + SM_ROWS, :]
            kv_pos = lax.broadcasted_iota(jnp.int32, s.shape, 0) + (kv0 + r)
            q_pos = (lax.broadcasted_iota(jnp.int32, s.shape, 1) & (tq - 1)) + i * tq
            s = jnp.where(kv_pos <= q_pos, s, NEG_BIG)
            s_slots[slot][r:r + SM_ROWS, :] = s
            cm = jnp.max(s.reshape(SM_ROWS // 8, 8, nq), axis=0)
            bm = cm if bm is None else jnp.maximum(bm, cm)
        return bm

    def pair_body(pair):
        b = 2 * pair
        pending = jnp.maximum(b - 3, 0)
        softmax_begin(0, mb0_ref[...])
        softmax_group(0, 0, qk_group(b - 1, 1, 1) + pv_half(pending, 1, 0))
        softmax_group(0, 1, qk_group(b, 0, 0) + pv_half(pending, 1, 1))
        softmax_begin(1, mb1_ref[...])
        softmax_group(1, 0, qk_group(b, 0, 1) + pv_half(b - 2, 0, 0))
        softmax_group(1, 1, qk_group(b + 1, 1, 0) + pv_half(b - 2, 0, 1))

    for w in qk_group(0, 0, 0) + qk_group(0, 0, 1) + qk_group(1, 1, 0):
        w()

    rem = i & (PAIRS_PER_TRIP - 1)
    bit = 1
    while bit < PAIRS_PER_TRIP:
        @pl.when((rem & bit) != 0)
        def _(bit=bit):
            for t in range(bit):
                pair_body(1 + (rem & (bit - 1)) + t)
        bit *= 2

    def body(j, carry):
        for t in range(PAIRS_PER_TRIP):
            pair_body(1 + rem + PAIRS_PER_TRIP * j + t)
        return carry

    lax.fori_loop(0, i // PAIRS_PER_TRIP, body, 0)

    b0 = 2 * i
    pending = jnp.maximum(b0 - 1, 0)
    for w in qk_group(b0 + 1, 1, 1) + pv_half(pending, 1, 0) + pv_half(pending, 1, 1):
        w()
    for slot in range(2):
        kv0 = (b0 + slot) * tk
        softmax_begin(slot, mask_block(slot, kv0))
        for g in range(tk // QK_ROWS):
            softmax_group(slot, g, [])
        for w in pv_half(b0 + slot, slot, 0) + pv_half(b0 + slot, slot, 1):
            w()

    _attn_finalize(lam0_ref, lp_ref, gs_ref, acc_ref, o_ref)


def _store_branch_qt(q_ref, qt_ref):
    tq = ATTN_TQ
    zeros = jnp.zeros((HEAD_DIM, tq), _BF16)
    qt_ref[0:HEAD_DIM, 0:tq] = q_ref[0:HEAD_DIM, :]
    qt_ref[HEAD_DIM:V_DIM, 0:tq] = zeros
    qt_ref[0:HEAD_DIM, tq:2 * tq] = zeros
    qt_ref[HEAD_DIM:V_DIM, tq:2 * tq] = q_ref[HEAD_DIM:V_DIM, :]


def _attn_finalize(lam0_ref, lp_ref, gs_ref, acc_ref, o_ref):
    tq = ATTN_TQ
    lp = lp_ref[...]
    lam0 = lam0_ref[0:1, 0:1]
    lam = (jnp.exp(jnp.sum(lp[0:1, :] * lp[1:2, :], axis=-1, keepdims=True))
           - jnp.exp(jnp.sum(lp[2:3, :] * lp[3:4, :], axis=-1, keepdims=True)) + lam0)
    o = acc_ref[0:V_DIM, :] * (1.0 / acc_ref[V_DIM:V_DIM + 1, :])
    a = o[:, 0:tq] - lam * o[:, tq:2 * tq]
    a = a * lax.rsqrt(jnp.mean(a * a, axis=0, keepdims=True) + EPS)
    o_ref[...] = (a.T * gs_ref[...] * (1.0 - lam0)).astype(o_ref.dtype)


def _attn(lam0, lp, gs, qt, k, vt):
    s = k.shape[0]
    tq, tk = ATTN_TQ, ATTN_TK
    nq = 2 * tq
    return pl.pallas_call(
        _attn_kernel,
        grid=(ATTN_HEADS, s // tq),
        in_specs=[pl.BlockSpec((1, 128), lambda h, i: (0, 0)),
                  pl.BlockSpec((4, HEAD_DIM), lambda h, i: (0, 0)),
                  pl.BlockSpec((1, V_DIM), lambda h, i: (0, 0)),
                  pl.BlockSpec((V_DIM, tq), lambda h, i: (h, i)),
                  pl.BlockSpec((s, V_DIM), lambda h, i: (0, h)),
                  pl.BlockSpec((V_DIM, s), lambda h, i: (h, 0))],
        out_specs=pl.BlockSpec((tq, V_DIM), lambda h, i: (i, h)),
        out_shape=jax.ShapeDtypeStruct((s, ATTN_WIDTH), _BF16),
        scratch_shapes=[pltpu.VMEM((V_DIM, nq), _BF16),
                        pltpu.VMEM((tk, nq), _F32),
                        pltpu.VMEM((tk, nq), _F32),
                        pltpu.VMEM((8, nq), _F32),
                        pltpu.VMEM((8, nq), _F32),
                        pltpu.VMEM((tk, nq), _BF16),
                        pltpu.VMEM((tk, nq), _BF16),
                        pltpu.VMEM((8, nq), _F32),
                        pltpu.VMEM((8, nq), _F32),
                        pltpu.VMEM((8, nq), _F32),
                        pltpu.VMEM((ACC_ROWS, nq), _F32)],
        compiler_params=pltpu.CompilerParams(dimension_semantics=("arbitrary", "arbitrary"),
                                             vmem_limit_bytes=VMEM_LIMIT),
        name="attn",
    )(lam0, lp, gs, qt, k, vt)


def _mix_kernel(x_ref, a_ref, xp_ref, halo_ref, gate_ref, pw_ref, ps_ref,
                wua_ref, wup_ref, wo_ref, o_ref):
    tm = ROW_TILE
    i = pl.program_id(0)

    halo = jnp.where(i > 0, halo_ref[...], 0.0)
    ext = jnp.concatenate([halo, xp_ref[...]], axis=0)
    t1 = (lax.broadcasted_iota(jnp.int32, (tm, POOL_GROUP_DIM), 0) + (i * tm + 1)).astype(_F32)
    sums = ext
    pooled = []
    shift = 1
    for g, w in enumerate(POOL_WINDOWS):
        while shift < w:
            sums = sums + pltpu.roll(sums, shift, axis=0)
            shift *= 2
        lo = g * POOL_GROUP_DIM
        win = sums[POOL_HALO:, lo:lo + POOL_GROUP_DIM]
        cur = ext[POOL_HALO:, lo:lo + POOL_GROUP_DIM]
        pooled.append(win / jnp.minimum(t1, float(w)) - cur)
    mixed = [jnp.dot(pooled[g].astype(_BF16), pw_ref[g], preferred_element_type=_F32)
             for g in range(POOL_GROUPS)]
    p = (jnp.concatenate(mixed, axis=-1) * ps_ref[...]).astype(_BF16)

    y_attn = jnp.dot(a_ref[...], wua_ref[...], preferred_element_type=_F32)
    y_pool = jnp.dot(p, wup_ref[...], preferred_element_type=_F32)
    gates = gate_ref[...].astype(_F32)
    merged = gates[:, :D_MODEL] * y_attn + gates[:, D_MODEL:] * y_pool
    o_ref[...] = x_ref[...] + jnp.dot(merged.astype(_BF16), wo_ref[...], preferred_element_type=_F32)


def _mix(x, a, xp, gates, pool_w, pool_scale, w_up_attn, w_up_pool, w_o):
    s = x.shape[0]
    tm = ROW_TILE
    row = lambda w: pl.BlockSpec((tm, w), lambda i: (i, 0))
    halo_blocks = tm // POOL_HALO
    halo = pl.BlockSpec((POOL_HALO, POOL_WIDTH), lambda i: (jnp.maximum(i * halo_blocks - 1, 0), 0))
    return pl.pallas_call(
        _mix_kernel,
        grid=(s // tm,),
        in_specs=[row(D_MODEL), row(ATTN_WIDTH), row(POOL_WIDTH), halo, row(GATE_WIDTH),
                  _const_spec((POOL_GROUPS, POOL_GROUP_DIM, POOL_GROUP_DIM)), _const_spec((1, POOL_WIDTH)),
                  _const_spec((ATTN_WIDTH, D_MODEL)), _const_spec((POOL_WIDTH, D_MODEL)),
                  _const_spec((D_MODEL, D_MODEL))],
        out_specs=row(D_MODEL),
        out_shape=jax.ShapeDtypeStruct((s, D_MODEL), _F32),
        compiler_params=pltpu.CompilerParams(dimension_semantics=("arbitrary",),
                                             vmem_limit_bytes=VMEM_LIMIT),
        name="mix",
    )(x, a, xp, xp, gates, pool_w, pool_scale, w_up_attn, w_up_pool, w_o)


def _mlp_kernel(x_ref, g2_ref, w1_ref, w2_ref, o_ref):
    x = x_ref[...]
    h = (x * _rms_scale(x) * g2_ref[...]).astype(_BF16)
    acc = x
    for c in range(D_FF // FF_CHUNK):
        lo = c * FF_CHUNK
        u = jnp.dot(h, w1_ref[:, lo:lo + FF_CHUNK], preferred_element_type=_F32)
        u = jnp.square(jnp.maximum(u, 0.0)).astype(_BF16)
        acc = acc + jnp.dot(u, w2_ref[lo:lo + FF_CHUNK, :], preferred_element_type=_F32)
    o_ref[...] = acc


def _mlp(x, g2, w1, w2):
    s = x.shape[0]
    tm = ROW_TILE
    row = pl.BlockSpec((tm, D_MODEL), lambda i: (i, 0))
    return pl.pallas_call(
        _mlp_kernel,
        grid=(s // tm,),
        in_specs=[row, _const_spec((1, D_MODEL)), _const_spec((D_MODEL, D_FF)), _const_spec((D_FF, D_MODEL))],
        out_specs=row,
        out_shape=jax.ShapeDtypeStruct((s, D_MODEL), _F32),
        compiler_params=pltpu.CompilerParams(dimension_semantics=("arbitrary",),
                                             vmem_limit_bytes=VMEM_LIMIT),
        name="mlp",
    )(x, g2, w1, w2)


def _lambda_init(layer_idx):
    return 0.8 - 0.6 * math.exp(-0.3 * layer_idx)


def kernel(x, norm1_g, w_in, b_gate, q_norm_g, k_norm_g, lam_params, subln_g, pool_w, pool_scale,
           w_up_attn, w_up_pool, w_o, norm2_g, w_mlp_in, w_mlp_out):
    b, s, d = x.shape
    assert (b, d) == (1, D_MODEL) and s % ROW_TILE == 0 and s % ATTN_TQ == 0
    xs = x.reshape(s, d)
    lane_group = jnp.arange(QK_WIDTH) // HEAD_DIM
    seg = jnp.where(lane_group[:, None] == lane_group[None, :], 1.0 / HEAD_DIM, 0.0).astype(_BF16)
    n_rep = QK_WIDTH // HEAD_DIM
    for l in range(DEPTH):
        gq = jnp.tile(q_norm_g[l], n_rep)[None, :] * (HEAD_DIM ** -0.5 * math.log2(math.e))
        gk = jnp.tile(k_norm_g[l], n_rep)[None, :]
        qt, k, vt, xp, gates = _proj(xs, norm1_g[l][None, :], w_in[l].astype(_BF16), b_gate[l][None, :],
                                   gq, gk, seg)
        lam0 = jnp.full((1, 128), _lambda_init(l), _F32)
        a = _attn(lam0, lam_params[l], subln_g[l][None, :], qt, k, vt)
        xs = _mix(xs, a, xp, gates, pool_w[l].astype(_BF16), pool_scale[l][None, :],
                  w_up_attn[l].astype(_BF16), w_up_pool[l].astype(_BF16), w_o[l].astype(_BF16))
        xs = _mlp(xs, norm2_g[l][None, :], w_mlp_in[l].astype(_BF16), w_mlp_out[l].astype(_BF16))
    return xs.reshape(b, s, d)
```

```python
import functools
import math

import jax
import jax.numpy as jnp
from jax import lax
from jax.experimental import pallas as pl
from jax.experimental.pallas import tpu as pltpu

D_MODEL = 1024
DEPTH = 4
ATTN_HEADS = 4
HEAD_DIM = 64
V_DIM = 2 * HEAD_DIM
ATTN_WIDTH = ATTN_HEADS * V_DIM
QK_WIDTH = ATTN_HEADS * 2 * HEAD_DIM
POOL_WINDOWS = (2, 4, 8, 16)
POOL_GROUPS = len(POOL_WINDOWS)
POOL_GROUP_DIM = 128
POOL_WIDTH = POOL_GROUPS * POOL_GROUP_DIM
GATE_WIDTH = 2 * D_MODEL
IN_COLS = 2 * QK_WIDTH + ATTN_WIDTH + POOL_WIDTH + GATE_WIDTH
D_FF = 4 * D_MODEL
EPS = 1e-6
NEG_BIG = -1e30

POOL_HALO = 16
ROW_TILE = 1024
ATTN_TQ = 512
ATTN_TK = ATTN_TQ // 2
QK_ROWS = 128
MXU_COLS = 256
SM_ROWS = 32
SUM_ROWS = 16
ACC_ROWS = V_DIM + SUM_ROWS
PAIRS_PER_TRIP = 8
FF_CHUNK = 1024
VMEM_LIMIT = 56 * 1024 * 1024

_BF16 = jnp.bfloat16
_F32 = jnp.float32


def _const_spec(shape):
    return pl.BlockSpec(shape, lambda *_: (0,) * len(shape), pipeline_mode=pl.Buffered(1))


def _layer_spec(shape, layer):
    return pl.BlockSpec((None,) + shape, lambda *_: (layer,) + (0,) * len(shape),
                        pipeline_mode=pl.Buffered(1))


def _rms_scale(x):
    return lax.rsqrt(jnp.mean(x * x, axis=-1, keepdims=True) + EPS)


def _proj_kernel(x_ref, g1_ref, w_ref, bg_ref, gq_ref, gk_ref, seg_ref,
                 qt_ref, k_ref, vt_ref, xp_ref, gate_ref):
    x = x_ref[...]
    h = (x * _rms_scale(x) * g1_ref[...]).astype(_BF16)

    def cols(lo, hi):
        return jnp.dot(h, w_ref[:, lo:hi], preferred_element_type=_F32)

    def qk_norm(t, g_ref):
        ms = jnp.dot((t * t).astype(_BF16), seg_ref[...], preferred_element_type=_F32)
        return t * lax.rsqrt(ms + EPS) * g_ref[...]

    def store_head_major(dst_ref, t):
        for hd in range(ATTN_HEADS):
            dst_ref[hd * V_DIM:(hd + 1) * V_DIM, :] = t[:, hd * V_DIM:(hd + 1) * V_DIM].T.astype(_BF16)

    c_q, c_k = QK_WIDTH, 2 * QK_WIDTH
    c_v, c_p = c_k + ATTN_WIDTH, c_k + ATTN_WIDTH + POOL_WIDTH
    store_head_major(qt_ref, qk_norm(cols(0, c_q), gq_ref))
    k_ref[...] = qk_norm(cols(c_q, c_k), gk_ref).astype(_BF16)
    store_head_major(vt_ref, cols(c_k, c_v))
    xp_ref[...] = cols(c_v, c_p)
    gate_ref[...] = jax.nn.sigmoid(cols(c_p, IN_COLS) + bg_ref[...]).astype(_BF16)


def _proj(layer, x, g1, w_in, b_gate, gq, gk, seg):
    s = x.shape[0]
    tm = ROW_TILE
    row = lambda w: pl.BlockSpec((tm, w), lambda i: (i, 0))
    return pl.pallas_call(
        _proj_kernel,
        grid=(s // tm,),
        in_specs=[row(D_MODEL), _const_spec((1, D_MODEL)), _layer_spec((D_MODEL, IN_COLS), layer),
                  _const_spec((1, GATE_WIDTH)), _const_spec((1, QK_WIDTH)), _const_spec((1, QK_WIDTH)),
                  _const_spec((QK_WIDTH, QK_WIDTH))],
        out_specs=[pl.BlockSpec((QK_WIDTH, tm), lambda i: (0, i)), row(QK_WIDTH),
                   pl.BlockSpec((ATTN_WIDTH, tm), lambda i: (0, i)), row(POOL_WIDTH), row(GATE_WIDTH)],
        out_shape=[jax.ShapeDtypeStruct((QK_WIDTH, s), _BF16), jax.ShapeDtypeStruct((s, QK_WIDTH), _BF16),
                   jax.ShapeDtypeStruct((ATTN_WIDTH, s), _BF16), jax.ShapeDtypeStruct((s, POOL_WIDTH), _F32),
                   jax.ShapeDtypeStruct((s, GATE_WIDTH), _BF16)],
        compiler_params=pltpu.CompilerParams(dimension_semantics=("arbitrary",),
                                             vmem_limit_bytes=VMEM_LIMIT),
        name="proj",
    )(x, g1, w_in, b_gate, gq, gk, seg)


def _attn_kernel(lam0_ref, lp_ref, gs_ref, q_ref, k_ref, vt_ref, o_ref,
                 qt_ref, s0_ref, s1_ref, mb0_ref, mb1_ref, p0_ref, p1_ref, al0_ref, al1_ref,
                 m_ref, acc_ref):
    tq, tk = ATTN_TQ, ATTN_TK
    nq = 2 * tq
    i = pl.program_id(1)
    s_slots, mb_slots = (s0_ref, s1_ref), (mb0_ref, mb1_ref)
    p_slots, al_slots = (p0_ref, p1_ref), (al0_ref, al1_ref)
    n_ctiles = nq // MXU_COLS
    n_sm = QK_ROWS // SM_ROWS

    _store_branch_qt(q_ref, qt_ref)
    m_ref[...] = jnp.full(m_ref.shape, NEG_BIG, _F32)
    acc_ref[...] = jnp.zeros(acc_ref.shape, _F32)
    p1_ref[...] = jnp.zeros(p1_ref.shape, _BF16)
    al1_ref[...] = jnp.ones(al1_ref.shape, _F32)

    def aligned(start, mult):
        return start if isinstance(start, int) else pl.multiple_of(start, mult)

    def col(c):
        return slice(c * MXU_COLS, (c + 1) * MXU_COLS)

    def qk_chunk(b, slot, g, c):
        row0 = g * QK_ROWS
        k = k_ref[pl.ds(aligned(b * tk + row0, QK_ROWS), QK_ROWS), :]
        s = jnp.dot(k, qt_ref[:, col(c)], preferred_element_type=_F32)
        s_slots[slot][row0:row0 + QK_ROWS, col(c)] = s
        cmax = jnp.max(s.reshape(QK_ROWS // 8, 8, MXU_COLS), axis=0)
        if g > 0:
            cmax = jnp.maximum(mb_slots[slot][:, col(c)], cmax)
        mb_slots[slot][:, col(c)] = cmax

    def qk_group(b, slot, g):
        return [functools.partial(qk_chunk, b, slot, g, c) for c in range(n_ctiles)]

    def pv_chunk(b, slot, c):
        vt = vt_ref[:, pl.ds(aligned(b * tk, tk), tk)]
        lhs = jnp.concatenate([vt, jnp.ones((SUM_ROWS, tk), _BF16)], axis=0)
        res = jnp.dot(lhs, p_slots[slot][:, col(c)], preferred_element_type=_F32)
        acc = acc_ref[:, col(c)].reshape(ACC_ROWS // 8, 8, MXU_COLS)
        acc = acc * al_slots[slot][:, col(c)][None] + res.reshape(ACC_ROWS // 8, 8, MXU_COLS)
        acc_ref[:, col(c)] = acc.reshape(ACC_ROWS, MXU_COLS)

    def pv_half(b, slot, half):
        per = n_ctiles // 2
        return [functools.partial(pv_chunk, b, slot, c) for c in range(half * per, (half + 1) * per)]

    def softmax_begin(slot, blk_max8):
        m_prev = m_ref[...]
        m_new = jnp.maximum(m_prev, jnp.max(blk_max8, axis=0, keepdims=True))
        alpha = jnp.exp2(m_prev - m_new)
        m_ref[...] = m_new
        al_slots[slot][...] = alpha

    def softmax_chunk(slot, r):
        s = s_slots[slot][r:r + SM_ROWS, :].reshape(SM_ROWS // 8, 8, nq)
        p = jnp.exp2(s - m_ref[...][None])
        p_slots[slot][r:r + SM_ROWS, :] = p.reshape(SM_ROWS, nq).astype(_BF16)

    def softmax_group(slot, g, mxu_work):
        n_w = len(mxu_work)
        for c in range(n_sm):
            for w in mxu_work[-(-c * n_w // n_sm):-(-(c + 1) * n_w // n_sm)]:
                w()
            softmax_chunk(slot, g * QK_ROWS + c * SM_ROWS)

    def mask_block(slot, kv0):
        bm = None
        for r in range(0, tk, SM_ROWS):
            s = s_slots[slot][r:r + SM_ROWS, :]
            kv_pos = lax.broadcasted_iota(jnp.int32, s.shape, 0) + (kv0 + r)
            q_pos = (lax.broadcasted_iota(jnp.int32, s.shape, 1) & (tq - 1)) + i * tq
            s = jnp.where(kv_pos <= q_pos, s, NEG_BIG)
            s_slots[slot][r:r + SM_ROWS, :] = s
            cm = jnp.max(s.reshape(SM_ROWS // 8, 8, nq), axis=0)
            bm = cm if bm is None else jnp.maximum(bm, cm)
        return bm

    def pair_body(pair):
        b = 2 * pair
        pending = jnp.maximum(b - 3, 0)
        softmax_begin(0, mb0_ref[...])
        softmax_group(0, 0, qk_group(b - 1, 1, 1) + pv_half(pending, 1, 0))
        softmax_group(0, 1, qk_group(b, 0, 0) + pv_half(pending, 1, 1))
        softmax_begin(1, mb1_ref[...])
        softmax_group(1, 0, qk_group(b, 0, 1) + pv_half(b - 2, 0, 0))
        softmax_group(1, 1, qk_group(b + 1, 1, 0) + pv_half(b - 2, 0, 1))

    for w in qk_group(0, 0, 0) + qk_group(0, 0, 1) + qk_group(1, 1, 0):
        w()

    rem = i & (PAIRS_PER_TRIP - 1)
    bit = 1
    while bit < PAIRS_PER_TRIP:
        @pl.when((rem & bit) != 0)
        def _(bit=bit):
            for t in range(bit):
                pair_body(1 + (rem & (bit - 1)) + t)
        bit *= 2

    def body(j, carry):
        for t in range(PAIRS_PER_TRIP):
            pair_body(1 + rem + PAIRS_PER_TRIP * j + t)
        return carry

    lax.fori_loop(0, i // PAIRS_PER_TRIP, body, 0)

    b0 = 2 * i
    pending = jnp.maximum(b0 - 1, 0)
    for w in qk_group(b0 + 1, 1, 1) + pv_half(pending, 1, 0) + pv_half(pending, 1, 1):
        w()
    for slot in range(2):
        kv0 = (b0 + slot) * tk
        softmax_begin(slot, mask_block(slot, kv0))
        for g in range(tk // QK_ROWS):
            softmax_group(slot, g, [])
        for w in pv_half(b0 + slot, slot, 0) + pv_half(b0 + slot, slot, 1):
            w()

    _attn_finalize(lam0_ref, lp_ref, gs_ref, acc_ref, o_ref)


def _store_branch_qt(q_ref, qt_ref):
    tq = ATTN_TQ
    zeros = jnp.zeros((HEAD_DIM, tq), _BF16)
    qt_ref[0:HEAD_DIM, 0:tq] = q_ref[0:HEAD_DIM, :]
    qt_ref[HEAD_DIM:V_DIM, 0:tq] = zeros
    qt_ref[0:HEAD_DIM, tq:2 * tq] = zeros
    qt_ref[HEAD_DIM:V_DIM, tq:2 * tq] = q_ref[HEAD_DIM:V_DIM, :]


def _attn_finalize(lam0_ref, lp_ref, gs_ref, acc_ref, o_ref):
    tq = ATTN_TQ
    lp = lp_ref[...]
    lam0 = lam0_ref[0:1, 0:1]
    lam = (jnp.exp(jnp.sum(lp[0:1, :] * lp[1:2, :], axis=-1, keepdims=True))
           - jnp.exp(jnp.sum(lp[2:3, :] * lp[3:4, :], axis=-1, keepdims=True)) + lam0)
    o = acc_ref[0:V_DIM, :] * (1.0 / acc_ref[V_DIM:V_DIM + 1, :])
    a = o[:, 0:tq] - lam * o[:, tq:2 * tq]
    a = a * lax.rsqrt(jnp.mean(a * a, axis=0, keepdims=True) + EPS)
    o_ref[...] = (a.T * gs_ref[...] * (1.0 - lam0)).astype(o_ref.dtype)


def _attn(lam0, lp, gs, qt, k, vt):
    s = k.shape[0]
    tq, tk = ATTN_TQ, ATTN_TK
    nq = 2 * tq
    return pl.pallas_call(
        _attn_kernel,
        grid=(ATTN_HEADS, s // tq),
        in_specs=[pl.BlockSpec((1, 128), lambda h, i: (0, 0)),
                  pl.BlockSpec((4, HEAD_DIM), lambda h, i: (0, 0)),
                  pl.BlockSpec((1, V_DIM), lambda h, i: (0, 0)),
                  pl.BlockSpec((V_DIM, tq), lambda h, i: (h, i)),
                  pl.BlockSpec((s, V_DIM), lambda h, i: (0, h)),
                  pl.BlockSpec((V_DIM, s), lambda h, i: (h, 0))],
        out_specs=pl.BlockSpec((tq, V_DIM), lambda h, i: (i, h)),
        out_shape=jax.ShapeDtypeStruct((s, ATTN_WIDTH), _BF16),
        scratch_shapes=[pltpu.VMEM((V_DIM, nq), _BF16),
                        pltpu.VMEM((tk, nq), _F32),
                        pltpu.VMEM((tk, nq), _F32),
                        pltpu.VMEM((8, nq), _F32),
                        pltpu.VMEM((8, nq), _F32),
                        pltpu.VMEM((tk, nq), _BF16),
                        pltpu.VMEM((tk, nq), _BF16),
                        pltpu.VMEM((8, nq), _F32),
                        pltpu.VMEM((8, nq), _F32),
                        pltpu.VMEM((8, nq), _F32),
                        pltpu.VMEM((ACC_ROWS, nq), _F32)],
        compiler_params=pltpu.CompilerParams(dimension_semantics=("arbitrary", "arbitrary"),
                                             vmem_limit_bytes=VMEM_LIMIT),
        name="attn",
    )(lam0, lp, gs, qt, k, vt)


def _mix_kernel(x_ref, a_ref, xp_ref, halo_ref, gate_ref, pw_ref, ps_ref,
                wua_ref, wup_ref, wo_ref, o_ref):
    tm = ROW_TILE
    i = pl.program_id(0)

    halo = jnp.where(i > 0, halo_ref[...], 0.0)
    ext = jnp.concatenate([halo, xp_ref[...]], axis=0)
    t1 = (lax.broadcasted_iota(jnp.int32, (tm, POOL_GROUP_DIM), 0) + (i * tm + 1)).astype(_F32)
    sums = ext
    pooled = []
    shift = 1
    for g, w in enumerate(POOL_WINDOWS):
        while shift < w:
            sums = sums + pltpu.roll(sums, shift, axis=0)
            shift *= 2
        lo = g * POOL_GROUP_DIM
        win = sums[POOL_HALO:, lo:lo + POOL_GROUP_DIM]
        cur = ext[POOL_HALO:, lo:lo + POOL_GROUP_DIM]
        pooled.append(win / jnp.minimum(t1, float(w)) - cur)
    mixed = [jnp.dot(pooled[g].astype(_BF16), pw_ref[g], preferred_element_type=_F32)
             for g in range(POOL_GROUPS)]
    p = (jnp.concatenate(mixed, axis=-1) * ps_ref[...]).astype(_BF16)

    y_attn = jnp.dot(a_ref[...], wua_ref[...], preferred_element_type=_F32)
    y_pool = jnp.dot(p, wup_ref[...], preferred_element_type=_F32)
    gates = gate_ref[...].astype(_F32)
    merged = gates[:, :D_MODEL] * y_attn + gates[:, D_MODEL:] * y_pool
    o_ref[...] = x_ref[...] + jnp.dot(merged.astype(_BF16), wo_ref[...], preferred_element_type=_F32)


def _mix(layer, x, a, xp, gates, pool_w, pool_scale, w_up_attn, w_up_pool, w_o):
    s = x.shape[0]
    tm = ROW_TILE
    row = lambda w: pl.BlockSpec((tm, w), lambda i: (i, 0))
    halo_blocks = tm // POOL_HALO
    halo = pl.BlockSpec((POOL_HALO, POOL_WIDTH), lambda i: (jnp.maximum(i * halo_blocks - 1, 0), 0))
    return pl.pallas_call(
        _mix_kernel,
        grid=(s // tm,),
        in_specs=[row(D_MODEL), row(ATTN_WIDTH), row(POOL_WIDTH), halo, row(GATE_WIDTH),
                  _layer_spec((POOL_GROUPS, POOL_GROUP_DIM, POOL_GROUP_DIM), layer),
                  _const_spec((1, POOL_WIDTH)),
                  _layer_spec((ATTN_WIDTH, D_MODEL), layer), _layer_spec((POOL_WIDTH, D_MODEL), layer),
                  _layer_spec((D_MODEL, D_MODEL), layer)],
        out_specs=row(D_MODEL),
        out_shape=jax.ShapeDtypeStruct((s, D_MODEL), _F32),
        compiler_params=pltpu.CompilerParams(dimension_semantics=("arbitrary",),
                                             vmem_limit_bytes=VMEM_LIMIT),
        name="mix",
    )(x, a, xp, xp, gates, pool_w, pool_scale, w_up_attn, w_up_pool, w_o)


def _mlp_kernel(x_ref, g2_ref, w1_ref, w2_ref, o_ref):
    x = x_ref[...]
    h = (x * _rms_scale(x) * g2_ref[...]).astype(_BF16)
    acc = x
    for c in range(D_FF // FF_CHUNK):
        lo = c * FF_CHUNK
        u = jnp.dot(h, w1_ref[:, lo:lo + FF_CHUNK], preferred_element_type=_F32)
        u = jnp.square(jnp.maximum(u, 0.0)).astype(_BF16)
        acc = acc + jnp.dot(u, w2_ref[lo:lo + FF_CHUNK, :], preferred_element_type=_F32)
    o_ref[...] = acc


def _mlp(layer, x, g2, w1, w2):
    s = x.shape[0]
    tm = ROW_TILE
    row = pl.BlockSpec((tm, D_MODEL), lambda i: (i, 0))
    return pl.pallas_call(
        _mlp_kernel,
        grid=(s // tm,),
        in_specs=[row, _const_spec((1, D_MODEL)), _layer_spec((D_MODEL, D_FF), layer),
                  _layer_spec((D_FF, D_MODEL), layer)],
        out_specs=row,
        out_shape=jax.ShapeDtypeStruct((s, D_MODEL), _F32),
        compiler_params=pltpu.CompilerParams(dimension_semantics=("arbitrary",),
                                             vmem_limit_bytes=VMEM_LIMIT),
        name="mlp",
    )(x, g2, w1, w2)


def _lambda_init(layer_idx):
    return 0.8 - 0.6 * math.exp(-0.3 * layer_idx)


def kernel(x, norm1_g, w_in, b_gate, q_norm_g, k_norm_g, lam_params, subln_g, pool_w, pool_scale,
           w_up_attn, w_up_pool, w_o, norm2_g, w_mlp_in, w_mlp_out):
    b, s, d = x.shape
    assert (b, d) == (1, D_MODEL) and s % ROW_TILE == 0 and s % ATTN_TQ == 0
    xs = x.reshape(s, d)
    lane_group = jnp.arange(QK_WIDTH) // HEAD_DIM
    seg = jnp.where(lane_group[:, None] == lane_group[None, :], 1.0 / HEAD_DIM, 0.0).astype(_BF16)
    n_rep = QK_WIDTH // HEAD_DIM
    w_in, pool_w, w_up_attn, w_up_pool, w_o, w_mlp_in, w_mlp_out = (
        w.astype(_BF16) for w in (w_in, pool_w, w_up_attn, w_up_pool, w_o, w_mlp_in, w_mlp_out))
    for l in range(DEPTH):
        gq = jnp.tile(q_norm_g[l], n_rep)[None, :] * (HEAD_DIM ** -0.5 * math.log2(math.e))
        gk = jnp.tile(k_norm_g[l], n_rep)[None, :]
        qt, k, vt, xp, gates = _proj(l, xs, norm1_g[l][None, :], w_in, b_gate[l][None, :], gq, gk, seg)
        lam0 = jnp.full((1, 128), _lambda_init(l), _F32)
        a = _attn(lam0, lam_params[l], subln_g[l][None, :], qt, k, vt)
        xs = _mix(l, xs, a, xp, gates, pool_w, pool_scale[l][None, :], w_up_attn, w_up_pool, w_o)
        xs = _mlp(l, xs, norm2_g[l][None, :], w_mlp_in, w_mlp_out)
    return xs.reshape(b, s, d)
```

```python
import functools
import math

import jax
import jax.numpy as jnp
from jax import lax
from jax.experimental import pallas as pl
from jax.experimental.pallas import tpu as pltpu

D_MODEL = 1024
DEPTH = 4
ATTN_HEADS = 4
HEAD_DIM = 64
V_DIM = 2 * HEAD_DIM
ATTN_WIDTH = ATTN_HEADS * V_DIM
QK_WIDTH = ATTN_HEADS * 2 * HEAD_DIM
POOL_WINDOWS = (2, 4, 8, 16)
POOL_GROUPS = len(POOL_WINDOWS)
POOL_GROUP_DIM = 128
POOL_WIDTH = POOL_GROUPS * POOL_GROUP_DIM
GATE_WIDTH = 2 * D_MODEL
IN_COLS = 2 * QK_WIDTH + ATTN_WIDTH + POOL_WIDTH + GATE_WIDTH
D_FF = 4 * D_MODEL
EPS = 1e-6
NEG_BIG = -1e30

POOL_HALO = 16
ROW_TILE = 1024
ATTN_TQ = 512
ATTN_TK = ATTN_TQ // 2
QK_ROWS = 128
MXU_COLS = 256
SM_ROWS = 32
SUM_ROWS = 16
ACC_ROWS = V_DIM + SUM_ROWS
PAIRS_PER_TRIP = 8
FF_CHUNK = 1024
VMEM_LIMIT = 56 * 1024 * 1024

_BF16 = jnp.bfloat16
_F32 = jnp.float32


def _const_spec(shape):
    return pl.BlockSpec(shape, lambda *_: (0,) * len(shape), pipeline_mode=pl.Buffered(1))


def _layer_spec(shape, layer):
    return pl.BlockSpec((None,) + shape, lambda *_: (layer,) + (0,) * len(shape),
                        pipeline_mode=pl.Buffered(1))


def _rms_scale(x):
    return lax.rsqrt(jnp.mean(x * x, axis=-1, keepdims=True) + EPS)


def _proj_kernel(x_ref, g1_ref, w_ref, bg_ref, gq_ref, gk_ref, seg_ref,
                 qt_ref, k_ref, vt_ref, xp_ref, gate_ref):
    x = x_ref[...]
    h = (x * _rms_scale(x) * g1_ref[...]).astype(_BF16)

    def cols(lo, hi):
        return jnp.dot(h, w_ref[:, lo:hi], preferred_element_type=_F32)

    def qk_norm(t, g_ref):
        ms = jnp.dot((t * t).astype(_BF16), seg_ref[...], preferred_element_type=_F32)
        return t * lax.rsqrt(ms + EPS) * g_ref[...]

    def store_head_major(dst_ref, t):
        for hd in range(ATTN_HEADS):
            dst_ref[hd * V_DIM:(hd + 1) * V_DIM, :] = t[:, hd * V_DIM:(hd + 1) * V_DIM].T.astype(_BF16)

    c_q, c_k = QK_WIDTH, 2 * QK_WIDTH
    c_v, c_p = c_k + ATTN_WIDTH, c_k + ATTN_WIDTH + POOL_WIDTH
    store_head_major(qt_ref, qk_norm(cols(0, c_q), gq_ref))
    k_ref[...] = qk_norm(cols(c_q, c_k), gk_ref).astype(_BF16)
    store_head_major(vt_ref, cols(c_k, c_v))
    xp_ref[...] = cols(c_v, c_p)
    gate_ref[...] = jax.nn.sigmoid(cols(c_p, IN_COLS) + bg_ref[...]).astype(_BF16)


def _proj(layer, x, g1, w_in, b_gate, gq, gk, seg):
    s = x.shape[0]
    tm = ROW_TILE
    row = lambda w: pl.BlockSpec((tm, w), lambda i: (i, 0))
    return pl.pallas_call(
        _proj_kernel,
        grid=(s // tm,),
        in_specs=[row(D_MODEL), _const_spec((1, D_MODEL)), _layer_spec((D_MODEL, IN_COLS), layer),
                  _const_spec((1, GATE_WIDTH)), _const_spec((1, QK_WIDTH)), _const_spec((1, QK_WIDTH)),
                  _const_spec((QK_WIDTH, QK_WIDTH))],
        out_specs=[pl.BlockSpec((QK_WIDTH, tm), lambda i: (0, i)), row(QK_WIDTH),
                   pl.BlockSpec((ATTN_WIDTH, tm), lambda i: (0, i)), row(POOL_WIDTH), row(GATE_WIDTH)],
        out_shape=[jax.ShapeDtypeStruct((QK_WIDTH, s), _BF16), jax.ShapeDtypeStruct((s, QK_WIDTH), _BF16),
                   jax.ShapeDtypeStruct((ATTN_WIDTH, s), _BF16), jax.ShapeDtypeStruct((s, POOL_WIDTH), _F32),
                   jax.ShapeDtypeStruct((s, GATE_WIDTH), _BF16)],
        compiler_params=pltpu.CompilerParams(dimension_semantics=("arbitrary",),
                                             vmem_limit_bytes=VMEM_LIMIT),
        name="proj",
    )(x, g1, w_in, b_gate, gq, gk, seg)


def _attn_kernel(lam0_ref, lp_ref, gs_ref, q_ref, k_ref, vt_ref, o_ref,
                 qt_ref, s0_ref, s1_ref, mb0_ref, mb1_ref, p0_ref, p1_ref, al0_ref, al1_ref,
                 m_ref, acc_ref):
    tq, tk = ATTN_TQ, ATTN_TK
    nq = 2 * tq
    i = pl.program_id(1)
    s_slots, mb_slots = (s0_ref, s1_ref), (mb0_ref, mb1_ref)
    p_slots, al_slots = (p0_ref, p1_ref), (al0_ref, al1_ref)
    n_ctiles = nq // MXU_COLS
    n_sm = QK_ROWS // SM_ROWS

    _store_branch_qt(q_ref, qt_ref)
    m_ref[...] = jnp.full(m_ref.shape, NEG_BIG, _F32)
    acc_ref[...] = jnp.zeros(acc_ref.shape, _F32)
    p1_ref[...] = jnp.zeros(p1_ref.shape, _BF16)
    al1_ref[...] = jnp.ones(al1_ref.shape, _F32)

    def aligned(start, mult):
        return start if isinstance(start, int) else pl.multiple_of(start, mult)

    def col(c):
        return slice(c * MXU_COLS, (c + 1) * MXU_COLS)

    def qk_chunk(b, slot, g, c):
        row0 = g * QK_ROWS
        k = k_ref[pl.ds(aligned(b * tk + row0, QK_ROWS), QK_ROWS), :]
        s = jnp.dot(k, qt_ref[:, col(c)], preferred_element_type=_F32)
        s_slots[slot][row0:row0 + QK_ROWS, col(c)] = s
        cmax = jnp.max(s.reshape(QK_ROWS // 8, 8, MXU_COLS), axis=0)
        if g > 0:
            cmax = jnp.maximum(mb_slots[slot][:, col(c)], cmax)
        mb_slots[slot][:, col(c)] = cmax

    def qk_group(b, slot, g):
        return [functools.partial(qk_chunk, b, slot, g, c) for c in range(n_ctiles)]

    def pv_chunk(b, slot, c):
        vt = vt_ref[:, pl.ds(aligned(b * tk, tk), tk)]
        lhs = jnp.concatenate([vt, jnp.ones((SUM_ROWS, tk), _BF16)], axis=0)
        res = jnp.dot(lhs, p_slots[slot][:, col(c)], preferred_element_type=_F32)
        acc = acc_ref[:, col(c)].reshape(ACC_ROWS // 8, 8, MXU_COLS)
        acc = acc * al_slots[slot][:, col(c)][None] + res.reshape(ACC_ROWS // 8, 8, MXU_COLS)
        acc_ref[:, col(c)] = acc.reshape(ACC_ROWS, MXU_COLS)

    def pv_half(b, slot, half):
        per = n_ctiles // 2
        return [functools.partial(pv_chunk, b, slot, c) for c in range(half * per, (half + 1) * per)]

    def softmax_begin(slot, blk_max8):
        m_prev = m_ref[...]
        m_new = jnp.maximum(m_prev, jnp.max(blk_max8, axis=0, keepdims=True))
        alpha = jnp.exp2(m_prev - m_new)
        m_ref[...] = m_new
        al_slots[slot][...] = alpha

    def softmax_chunk(slot, r):
        s = s_slots[slot][r:r + SM_ROWS, :].reshape(SM_ROWS // 8, 8, nq)
        p = jnp.exp2(s - m_ref[...][None])
        p_slots[slot][r:r + SM_ROWS, :] = p.reshape(SM_ROWS, nq).astype(_BF16)

    def softmax_group(slot, g, mxu_work):
        n_w = len(mxu_work)
        for c in range(n_sm):
            for w in mxu_work[-(-c * n_w // n_sm):-(-(c + 1) * n_w // n_sm)]:
                w()
            softmax_chunk(slot, g * QK_ROWS + c * SM_ROWS)

    assert tk == MXU_COLS and tq == 2 * tk

    def diag_tiles(x, kinds):
        return [c for c in range(n_ctiles) if ("none", "tri", "all")[(c % 2 >= x) + (c % 2 > x)] in kinds]

    def mask_block(slot, x):
        bms = []
        for c in range(n_ctiles):
            if c in diag_tiles(x, ("none",)):
                bms.append(jnp.full((8, MXU_COLS), NEG_BIG, _F32))
                continue
            bm = None
            for r in range(0, tk, SM_ROWS):
                s = s_slots[slot][r:r + SM_ROWS, col(c)]
                if c in diag_tiles(x, ("tri",)):
                    key = lax.broadcasted_iota(jnp.int32, s.shape, 0) + r
                    s = jnp.where(key <= lax.broadcasted_iota(jnp.int32, s.shape, 1), s, NEG_BIG)
                    s_slots[slot][r:r + SM_ROWS, col(c)] = s
                cm = jnp.max(s.reshape(SM_ROWS // 8, 8, MXU_COLS), axis=0)
                bm = cm if bm is None else jnp.maximum(bm, cm)
            bms.append(bm)
        return jnp.concatenate(bms, axis=1)

    def softmax_tiles(slot, tiles):
        for r in range(0, tk, SM_ROWS):
            for c in tiles:
                s = s_slots[slot][r:r + SM_ROWS, col(c)].reshape(SM_ROWS // 8, 8, MXU_COLS)
                p = jnp.exp2(s - m_ref[:, col(c)][None])
                p_slots[slot][r:r + SM_ROWS, col(c)] = p.reshape(SM_ROWS, MXU_COLS).astype(_BF16)

    def pair_body(pair):
        b = 2 * pair
        pending = jnp.maximum(b - 3, 0)
        softmax_begin(0, mb0_ref[...])
        softmax_group(0, 0, qk_group(b - 1, 1, 1) + pv_half(pending, 1, 0))
        softmax_group(0, 1, qk_group(b, 0, 0) + pv_half(pending, 1, 1))
        softmax_begin(1, mb1_ref[...])
        softmax_group(1, 0, qk_group(b, 0, 1) + pv_half(b - 2, 0, 0))
        softmax_group(1, 1, qk_group(b + 1, 1, 0) + pv_half(b - 2, 0, 1))

    for w in qk_group(0, 0, 0) + qk_group(0, 0, 1) + qk_group(1, 1, 0):
        w()

    rem = i & (PAIRS_PER_TRIP - 1)
    bit = 1
    while bit < PAIRS_PER_TRIP:
        @pl.when((rem & bit) != 0)
        def _(bit=bit):
            for t in range(bit):
                pair_body(1 + (rem & (bit - 1)) + t)
        bit *= 2

    def body(j, carry):
        for t in range(PAIRS_PER_TRIP):
            pair_body(1 + rem + PAIRS_PER_TRIP * j + t)
        return carry

    lax.fori_loop(0, i // PAIRS_PER_TRIP, body, 0)

    b0 = 2 * i
    pending = jnp.maximum(b0 - 1, 0)
    for c in diag_tiles(1, ("tri", "all")):
        qk_chunk(b0 + 1, 1, 1, c)
    for w in pv_half(pending, 1, 0) + pv_half(pending, 1, 1):
        w()
    for x in range(2):
        live = diag_tiles(x, ("tri", "all"))
        softmax_begin(x, mask_block(x, x))
        softmax_tiles(x, live)
        for c in live:
            pv_chunk(b0 + x, x, c)

    _attn_finalize(lam0_ref, lp_ref, gs_ref, acc_ref, o_ref)


def _store_branch_qt(q_ref, qt_ref):
    tq = ATTN_TQ
    zeros = jnp.zeros((HEAD_DIM, tq), _BF16)
    qt_ref[0:HEAD_DIM, 0:tq] = q_ref[0:HEAD_DIM, :]
    qt_ref[HEAD_DIM:V_DIM, 0:tq] = zeros
    qt_ref[0:HEAD_DIM, tq:2 * tq] = zeros
    qt_ref[HEAD_DIM:V_DIM, tq:2 * tq] = q_ref[HEAD_DIM:V_DIM, :]


def _attn_finalize(lam0_ref, lp_ref, gs_ref, acc_ref, o_ref):
    tq = ATTN_TQ
    lp = lp_ref[...]
    lam0 = lam0_ref[0:1, 0:1]
    lam = (jnp.exp(jnp.sum(lp[0:1, :] * lp[1:2, :], axis=-1, keepdims=True))
           - jnp.exp(jnp.sum(lp[2:3, :] * lp[3:4, :], axis=-1, keepdims=True)) + lam0)
    o = acc_ref[0:V_DIM, :] * (1.0 / acc_ref[V_DIM:V_DIM + 1, :])
    a = o[:, 0:tq] - lam * o[:, tq:2 * tq]
    a = a * lax.rsqrt(jnp.mean(a * a, axis=0, keepdims=True) + EPS)
    o_ref[...] = (a.T * gs_ref[...] * (1.0 - lam0)).astype(o_ref.dtype)


def _attn(lam0, lp, gs, qt, k, vt):
    s = k.shape[0]
    tq, tk = ATTN_TQ, ATTN_TK
    nq = 2 * tq
    return pl.pallas_call(
        _attn_kernel,
        grid=(ATTN_HEADS, s // tq),
        in_specs=[pl.BlockSpec((1, 128), lambda h, i: (0, 0)),
                  pl.BlockSpec((4, HEAD_DIM), lambda h, i: (0, 0)),
                  pl.BlockSpec((1, V_DIM), lambda h, i: (0, 0)),
                  pl.BlockSpec((V_DIM, tq), lambda h, i: (h, i)),
                  pl.BlockSpec((s, V_DIM), lambda h, i: (0, h)),
                  pl.BlockSpec((V_DIM, s), lambda h, i: (h, 0))],
        out_specs=pl.BlockSpec((tq, V_DIM), lambda h, i: (i, h)),
        out_shape=jax.ShapeDtypeStruct((s, ATTN_WIDTH), _BF16),
        scratch_shapes=[pltpu.VMEM((V_DIM, nq), _BF16),
                        pltpu.VMEM((tk, nq), _F32),
                        pltpu.VMEM((tk, nq), _F32),
                        pltpu.VMEM((8, nq), _F32),
                        pltpu.VMEM((8, nq), _F32),
                        pltpu.VMEM((tk, nq), _BF16),
                        pltpu.VMEM((tk, nq), _BF16),
                        pltpu.VMEM((8, nq), _F32),
                        pltpu.VMEM((8, nq), _F32),
                        pltpu.VMEM((8, nq), _F32),
                        pltpu.VMEM((ACC_ROWS, nq), _F32)],
        compiler_params=pltpu.CompilerParams(dimension_semantics=("arbitrary", "arbitrary"),
                                             vmem_limit_bytes=VMEM_LIMIT),
        name="attn",
    )(lam0, lp, gs, qt, k, vt)


def _mix_kernel(x_ref, a_ref, xp_ref, halo_ref, gate_ref, pw_ref, ps_ref,
                wua_ref, wup_ref, wo_ref, o_ref):
    tm = ROW_TILE
    i = pl.program_id(0)

    halo = jnp.where(i > 0, halo_ref[...], 0.0)
    ext = jnp.concatenate([halo, xp_ref[...]], axis=0)
    t1 = (lax.broadcasted_iota(jnp.int32, (tm, POOL_GROUP_DIM), 0) + (i * tm + 1)).astype(_F32)
    sums = ext
    pooled = []
    shift = 1
    for g, w in enumerate(POOL_WINDOWS):
        while shift < w:
            sums = sums + pltpu.roll(sums, shift, axis=0)
            shift *= 2
        lo = g * POOL_GROUP_DIM
        win = sums[POOL_HALO:, lo:lo + POOL_GROUP_DIM]
        cur = ext[POOL_HALO:, lo:lo + POOL_GROUP_DIM]
        pooled.append(win / jnp.minimum(t1, float(w)) - cur)
    mixed = [jnp.dot(pooled[g].astype(_BF16), pw_ref[g], preferred_element_type=_F32)
             for g in range(POOL_GROUPS)]
    p = (jnp.concatenate(mixed, axis=-1) * ps_ref[...]).astype(_BF16)

    y_attn = jnp.dot(a_ref[...], wua_ref[...], preferred_element_type=_F32)
    y_pool = jnp.dot(p, wup_ref[...], preferred_element_type=_F32)
    gates = gate_ref[...].astype(_F32)
    merged = gates[:, :D_MODEL] * y_attn + gates[:, D_MODEL:] * y_pool
    o_ref[...] = x_ref[...] + jnp.dot(merged.astype(_BF16), wo_ref[...], preferred_element_type=_F32)


def _mix(layer, x, a, xp, gates, pool_w, pool_scale, w_up_attn, w_up_pool, w_o):
    s = x.shape[0]
    tm = ROW_TILE
    row = lambda w: pl.BlockSpec((tm, w), lambda i: (i, 0))
    halo_blocks = tm // POOL_HALO
    halo = pl.BlockSpec((POOL_HALO, POOL_WIDTH), lambda i: (jnp.maximum(i * halo_blocks - 1, 0), 0))
    return pl.pallas_call(
        _mix_kernel,
        grid=(s // tm,),
        in_specs=[row(D_MODEL), row(ATTN_WIDTH), row(POOL_WIDTH), halo, row(GATE_WIDTH),
                  _layer_spec((POOL_GROUPS, POOL_GROUP_DIM, POOL_GROUP_DIM), layer),
                  _const_spec((1, POOL_WIDTH)),
                  _layer_spec((ATTN_WIDTH, D_MODEL), layer), _layer_spec((POOL_WIDTH, D_MODEL), layer),
                  _layer_spec((D_MODEL, D_MODEL), layer)],
        out_specs=row(D_MODEL),
        out_shape=jax.ShapeDtypeStruct((s, D_MODEL), _F32),
        compiler_params=pltpu.CompilerParams(dimension_semantics=("arbitrary",),
                                             vmem_limit_bytes=VMEM_LIMIT),
        name="mix",
    )(x, a, xp, xp, gates, pool_w, pool_scale, w_up_attn, w_up_pool, w_o)


def _mlp_kernel(x_ref, g2_ref, w1_ref, w2_ref, o_ref):
    x = x_ref[...]
    h = (x * _rms_scale(x) * g2_ref[...]).astype(_BF16)
    acc = x
    for c in range(D_FF // FF_CHUNK):
        lo = c * FF_CHUNK
        u = jnp.dot(h, w1_ref[:, lo:lo + FF_CHUNK], preferred_element_type=_F32)
        u = jnp.square(jnp.maximum(u, 0.0)).astype(_BF16)
        acc = acc + jnp.dot(u, w2_ref[lo:lo + FF_CHUNK, :], preferred_element_type=_F32)
    o_ref[...] = acc


def _mlp(layer, x, g2, w1, w2):
    s = x.shape[0]
    tm = ROW_TILE
    row = pl.BlockSpec((tm, D_MODEL), lambda i: (i, 0))
    return pl.pallas_call(
        _mlp_kernel,
        grid=(s // tm,),
        in_specs=[row, _const_spec((1, D_MODEL)), _layer_spec((D_MODEL, D_FF), layer),
                  _layer_spec((D_FF, D_MODEL), layer)],
        out_specs=row,
        out_shape=jax.ShapeDtypeStruct((s, D_MODEL), _F32),
        compiler_params=pltpu.CompilerParams(dimension_semantics=("arbitrary",),
                                             vmem_limit_bytes=VMEM_LIMIT),
        name="mlp",
    )(x, g2, w1, w2)


def _lambda_init(layer_idx):
    return 0.8 - 0.6 * math.exp(-0.3 * layer_idx)


def kernel(x, norm1_g, w_in, b_gate, q_norm_g, k_norm_g, lam_params, subln_g, pool_w, pool_scale,
           w_up_attn, w_up_pool, w_o, norm2_g, w_mlp_in, w_mlp_out):
    b, s, d = x.shape
    assert (b, d) == (1, D_MODEL) and s % ROW_TILE == 0 and s % ATTN_TQ == 0
    xs = x.reshape(s, d)
    lane_group = jnp.arange(QK_WIDTH) // HEAD_DIM
    seg = jnp.where(lane_group[:, None] == lane_group[None, :], 1.0 / HEAD_DIM, 0.0).astype(_BF16)
    n_rep = QK_WIDTH // HEAD_DIM
    w_in, pool_w, w_up_attn, w_up_pool, w_o, w_mlp_in, w_mlp_out = (
        w.astype(_BF16) for w in (w_in, pool_w, w_up_attn, w_up_pool, w_o, w_mlp_in, w_mlp_out))
    for l in range(DEPTH):
        gq = jnp.tile(q_norm_g[l], n_rep)[None, :] * (HEAD_DIM ** -0.5 * math.log2(math.e))
        gk = jnp.tile(k_norm_g[l], n_rep)[None, :]
        qt, k, vt, xp, gates = _proj(l, xs, norm1_g[l][None, :], w_in, b_gate[l][None, :], gq, gk, seg)
        lam0 = jnp.full((1, 128), _lambda_init(l), _F32)
        a = _attn(lam0, lam_params[l], subln_g[l][None, :], qt, k, vt)
        xs = _mix(l, xs, a, xp, gates, pool_w, pool_scale[l][None, :], w_up_attn, w_up_pool, w_o)
        xs = _mlp(l, xs, norm2_g[l][None, :], w_mlp_in, w_mlp_out)
    return xs.reshape(b, s, d)
```

```python
import functools
import math

import jax
import jax.numpy as jnp
from jax import lax
from jax.experimental import pallas as pl
from jax.experimental.pallas import tpu as pltpu

D_MODEL = 1024
DEPTH = 4
ATTN_HEADS = 4
HEAD_DIM = 64
V_DIM = 2 * HEAD_DIM
ATTN_WIDTH = ATTN_HEADS * V_DIM
QK_WIDTH = ATTN_HEADS * 2 * HEAD_DIM
POOL_WINDOWS = (2, 4, 8, 16)
POOL_GROUPS = len(POOL_WINDOWS)
POOL_GROUP_DIM = 128
POOL_WIDTH = POOL_GROUPS * POOL_GROUP_DIM
GATE_WIDTH = 2 * D_MODEL
IN_COLS = 2 * QK_WIDTH + ATTN_WIDTH + POOL_WIDTH + GATE_WIDTH
D_FF = 4 * D_MODEL
EPS = 1e-6
NEG_BIG = -1e30

POOL_HALO = 16
ROW_TILE = 1024
ATTN_TQ = 512
ATTN_TK = ATTN_TQ // 2
QK_ROWS = 128
MXU_COLS = 256
SM_ROWS = 32
SUM_ROWS = 16
ACC_ROWS = V_DIM + SUM_ROWS
PAIRS_PER_TRIP = 8
FF_CHUNK = 1024
VMEM_LIMIT = 56 * 1024 * 1024

_BF16 = jnp.bfloat16
_F32 = jnp.float32


def _const_spec(shape):
    return pl.BlockSpec(shape, lambda *_: (0,) * len(shape), pipeline_mode=pl.Buffered(1))


def _layer_spec(shape, layer):
    return pl.BlockSpec((None,) + shape, lambda *_: (layer,) + (0,) * len(shape),
                        pipeline_mode=pl.Buffered(1))


def _rms_scale(x):
    return lax.rsqrt(jnp.mean(x * x, axis=-1, keepdims=True) + EPS)


def _proj_kernel(x_ref, g1_ref, w_ref, bg_ref, gq_ref, gk_ref, seg_ref,
                 qt_ref, k_ref, vt_ref, xp_ref, gate_ref):
    x = x_ref[...]
    h = (x * _rms_scale(x) * g1_ref[...]).astype(_BF16)

    def cols(lo, hi):
        return jnp.dot(h, w_ref[:, lo:hi], preferred_element_type=_F32)

    def qk_norm(t, g_ref):
        ms = jnp.dot((t * t).astype(_BF16), seg_ref[...], preferred_element_type=_F32)
        return t * lax.rsqrt(ms + EPS) * g_ref[...]

    def store_head_major(dst_ref, t):
        for hd in range(ATTN_HEADS):
            dst_ref[hd * V_DIM:(hd + 1) * V_DIM, :] = t[:, hd * V_DIM:(hd + 1) * V_DIM].T.astype(_BF16)

    c_q, c_k = QK_WIDTH, 2 * QK_WIDTH
    c_v, c_p = c_k + ATTN_WIDTH, c_k + ATTN_WIDTH + POOL_WIDTH
    store_head_major(qt_ref, qk_norm(cols(0, c_q), gq_ref))
    k_ref[...] = qk_norm(cols(c_q, c_k), gk_ref).astype(_BF16)
    store_head_major(vt_ref, cols(c_k, c_v))
    xp_ref[...] = cols(c_v, c_p)
    gate_ref[...] = jax.nn.sigmoid(cols(c_p, IN_COLS) + bg_ref[...]).astype(_BF16)


def _proj(layer, x, g1, w_in, b_gate, gq, gk, seg):
    s = x.shape[0]
    tm = ROW_TILE
    row = lambda w: pl.BlockSpec((tm, w), lambda i: (i, 0))
    return pl.pallas_call(
        _proj_kernel,
        grid=(s // tm,),
        in_specs=[row(D_MODEL), _const_spec((1, D_MODEL)), _layer_spec((D_MODEL, IN_COLS), layer),
                  _const_spec((1, GATE_WIDTH)), _const_spec((1, QK_WIDTH)), _const_spec((1, QK_WIDTH)),
                  _const_spec((QK_WIDTH, QK_WIDTH))],
        out_specs=[pl.BlockSpec((QK_WIDTH, tm), lambda i: (0, i)), row(QK_WIDTH),
                   pl.BlockSpec((ATTN_WIDTH, tm), lambda i: (0, i)), row(POOL_WIDTH), row(GATE_WIDTH)],
        out_shape=[jax.ShapeDtypeStruct((QK_WIDTH, s), _BF16), jax.ShapeDtypeStruct((s, QK_WIDTH), _BF16),
                   jax.ShapeDtypeStruct((ATTN_WIDTH, s), _BF16), jax.ShapeDtypeStruct((s, POOL_WIDTH), _F32),
                   jax.ShapeDtypeStruct((s, GATE_WIDTH), _BF16)],
        compiler_params=pltpu.CompilerParams(dimension_semantics=("arbitrary",),
                                             vmem_limit_bytes=VMEM_LIMIT),
        name="proj",
    )(x, g1, w_in, b_gate, gq, gk, seg)


def _attn_kernel(*refs):
    n_blocks = refs[4].shape[0] // ATTN_TQ

    def block(i, carry):
        _attn_block(i, *refs)
        return carry

    lax.fori_loop(0, n_blocks, block, 0)


def _attn_block(i, lam0_ref, lp_ref, gs_ref, q_ref, k_ref, vt_ref, o_ref,
                qt_ref, s0_ref, s1_ref, mb0_ref, mb1_ref, p0_ref, p1_ref, al0_ref, al1_ref,
                m_ref, acc_ref):
    tq, tk = ATTN_TQ, ATTN_TK
    nq = 2 * tq
    s_slots, mb_slots = (s0_ref, s1_ref), (mb0_ref, mb1_ref)
    p_slots, al_slots = (p0_ref, p1_ref), (al0_ref, al1_ref)
    n_ctiles = nq // MXU_COLS
    n_sm = QK_ROWS // SM_ROWS

    _store_branch_qt(q_ref, qt_ref, i)
    m_ref[...] = jnp.full(m_ref.shape, NEG_BIG, _F32)
    acc_ref[...] = jnp.zeros(acc_ref.shape, _F32)
    p1_ref[...] = jnp.zeros(p1_ref.shape, _BF16)
    al1_ref[...] = jnp.ones(al1_ref.shape, _F32)

    def aligned(start, mult):
        return start if isinstance(start, int) else pl.multiple_of(start, mult)

    def col(c):
        return slice(c * MXU_COLS, (c + 1) * MXU_COLS)

    def qk_chunk(b, slot, g, c):
        row0 = g * QK_ROWS
        k = k_ref[pl.ds(aligned(b * tk + row0, QK_ROWS), QK_ROWS), :]
        s = jnp.dot(k, qt_ref[:, col(c)], preferred_element_type=_F32)
        s_slots[slot][row0:row0 + QK_ROWS, col(c)] = s
        cmax = jnp.max(s.reshape(QK_ROWS // 8, 8, MXU_COLS), axis=0)
        if g > 0:
            cmax = jnp.maximum(mb_slots[slot][:, col(c)], cmax)
        mb_slots[slot][:, col(c)] = cmax

    def qk_group(b, slot, g):
        return [functools.partial(qk_chunk, b, slot, g, c) for c in range(n_ctiles)]

    def pv_chunk(b, slot, c):
        vt = vt_ref[:, pl.ds(aligned(b * tk, tk), tk)]
        lhs = jnp.concatenate([vt, jnp.ones((SUM_ROWS, tk), _BF16)], axis=0)
        res = jnp.dot(lhs, p_slots[slot][:, col(c)], preferred_element_type=_F32)
        acc = acc_ref[:, col(c)].reshape(ACC_ROWS // 8, 8, MXU_COLS)
        acc = acc * al_slots[slot][:, col(c)][None] + res.reshape(ACC_ROWS // 8, 8, MXU_COLS)
        acc_ref[:, col(c)] = acc.reshape(ACC_ROWS, MXU_COLS)

    def pv_half(b, slot, half):
        per = n_ctiles // 2
        return [functools.partial(pv_chunk, b, slot, c) for c in range(half * per, (half + 1) * per)]

    def softmax_begin(slot, blk_max8):
        m_prev = m_ref[...]
        m_new = jnp.maximum(m_prev, jnp.max(blk_max8, axis=0, keepdims=True))
        alpha = jnp.exp2(m_prev - m_new)
        m_ref[...] = m_new
        al_slots[slot][...] = alpha

    def softmax_chunk(slot, r):
        s = s_slots[slot][r:r + SM_ROWS, :].reshape(SM_ROWS // 8, 8, nq)
        p = jnp.exp2(s - m_ref[...][None])
        p_slots[slot][r:r + SM_ROWS, :] = p.reshape(SM_ROWS, nq).astype(_BF16)

    def softmax_group(slot, g, mxu_work):
        n_w = len(mxu_work)
        for c in range(n_sm):
            for w in mxu_work[-(-c * n_w // n_sm):-(-(c + 1) * n_w // n_sm)]:
                w()
            softmax_chunk(slot, g * QK_ROWS + c * SM_ROWS)

    assert tk == MXU_COLS and tq == 2 * tk

    def diag_tiles(x, kinds):
        return [c for c in range(n_ctiles) if ("none", "tri", "all")[(c % 2 >= x) + (c % 2 > x)] in kinds]

    def mask_block(slot, x):
        bms = []
        for c in range(n_ctiles):
            if c in diag_tiles(x, ("none",)):
                bms.append(jnp.full((8, MXU_COLS), NEG_BIG, _F32))
                continue
            bm = None
            for r in range(0, tk, SM_ROWS):
                s = s_slots[slot][r:r + SM_ROWS, col(c)]
                if c in diag_tiles(x, ("tri",)):
                    key = lax.broadcasted_iota(jnp.int32, s.shape, 0) + r
                    s = jnp.where(key <= lax.broadcasted_iota(jnp.int32, s.shape, 1), s, NEG_BIG)
                    s_slots[slot][r:r + SM_ROWS, col(c)] = s
                cm = jnp.max(s.reshape(SM_ROWS // 8, 8, MXU_COLS), axis=0)
                bm = cm if bm is None else jnp.maximum(bm, cm)
            bms.append(bm)
        return jnp.concatenate(bms, axis=1)

    def softmax_tiles(slot, tiles):
        for r in range(0, tk, SM_ROWS):
            for c in tiles:
                s = s_slots[slot][r:r + SM_ROWS, col(c)].reshape(SM_ROWS // 8, 8, MXU_COLS)
                p = jnp.exp2(s - m_ref[:, col(c)][None])
                p_slots[slot][r:r + SM_ROWS, col(c)] = p.reshape(SM_ROWS, MXU_COLS).astype(_BF16)

    def pair_body(pair):
        b = 2 * pair
        pending = jnp.maximum(b - 3, 0)
        softmax_begin(0, mb0_ref[...])
        softmax_group(0, 0, qk_group(b - 1, 1, 1) + pv_half(pending, 1, 0))
        softmax_group(0, 1, qk_group(b, 0, 0) + pv_half(pending, 1, 1))
        softmax_begin(1, mb1_ref[...])
        softmax_group(1, 0, qk_group(b, 0, 1) + pv_half(b - 2, 0, 0))
        softmax_group(1, 1, qk_group(b + 1, 1, 0) + pv_half(b - 2, 0, 1))

    for w in qk_group(0, 0, 0) + qk_group(0, 0, 1) + qk_group(1, 1, 0):
        w()

    rem = i & (PAIRS_PER_TRIP - 1)
    bit = 1
    while bit < PAIRS_PER_TRIP:
        @pl.when((rem & bit) != 0)
        def _(bit=bit):
            for t in range(bit):
                pair_body(1 + (rem & (bit - 1)) + t)
        bit *= 2

    def body(j, carry):
        for t in range(PAIRS_PER_TRIP):
            pair_body(1 + rem + PAIRS_PER_TRIP * j + t)
        return carry

    lax.fori_loop(0, i // PAIRS_PER_TRIP, body, 0)

    b0 = 2 * i
    pending = jnp.maximum(b0 - 1, 0)
    for c in diag_tiles(1, ("tri", "all")):
        qk_chunk(b0 + 1, 1, 1, c)
    for w in pv_half(pending, 1, 0) + pv_half(pending, 1, 1):
        w()
    for x in range(2):
        live = diag_tiles(x, ("tri", "all"))
        softmax_begin(x, mask_block(x, x))
        softmax_tiles(x, live)
        for c in live:
            pv_chunk(b0 + x, x, c)

    _attn_finalize(lam0_ref, lp_ref, gs_ref, acc_ref, o_ref, i)


def _store_branch_qt(q_ref, qt_ref, i):
    tq = ATTN_TQ
    cols = pl.ds(pl.multiple_of(i * tq, tq), tq)
    zeros = jnp.zeros((HEAD_DIM, tq), _BF16)
    qt_ref[0:HEAD_DIM, 0:tq] = q_ref[0:HEAD_DIM, cols]
    qt_ref[HEAD_DIM:V_DIM, 0:tq] = zeros
    qt_ref[0:HEAD_DIM, tq:2 * tq] = zeros
    qt_ref[HEAD_DIM:V_DIM, tq:2 * tq] = q_ref[HEAD_DIM:V_DIM, cols]


def _attn_finalize(lam0_ref, lp_ref, gs_ref, acc_ref, o_ref, i):
    tq = ATTN_TQ
    lp = lp_ref[...]
    lam0 = lam0_ref[0:1, 0:1]
    lam = (jnp.exp(jnp.sum(lp[0:1, :] * lp[1:2, :], axis=-1, keepdims=True))
           - jnp.exp(jnp.sum(lp[2:3, :] * lp[3:4, :], axis=-1, keepdims=True)) + lam0)
    o = acc_ref[0:V_DIM, :] * (1.0 / acc_ref[V_DIM:V_DIM + 1, :])
    a = o[:, 0:tq] - lam * o[:, tq:2 * tq]
    a = a * lax.rsqrt(jnp.mean(a * a, axis=0, keepdims=True) + EPS)
    rows = pl.ds(pl.multiple_of(i * tq, tq), tq)
    o_ref[rows, :] = (a.T * gs_ref[...] * (1.0 - lam0)).astype(o_ref.dtype)


def _attn(lam0, lp, gs, qt, k, vt):
    s = k.shape[0]
    tq, tk = ATTN_TQ, ATTN_TK
    nq = 2 * tq
    return pl.pallas_call(
        _attn_kernel,
        grid=(ATTN_HEADS,),
        in_specs=[pl.BlockSpec((1, 128), lambda h: (0, 0)),
                  pl.BlockSpec((4, HEAD_DIM), lambda h: (0, 0)),
                  pl.BlockSpec((1, V_DIM), lambda h: (0, 0)),
                  pl.BlockSpec((V_DIM, s), lambda h: (h, 0)),
                  pl.BlockSpec((s, V_DIM), lambda h: (0, h)),
                  pl.BlockSpec((V_DIM, s), lambda h: (h, 0))],
        out_specs=pl.BlockSpec((s, V_DIM), lambda h: (0, h)),
        out_shape=jax.ShapeDtypeStruct((s, ATTN_WIDTH), _BF16),
        scratch_shapes=[pltpu.VMEM((V_DIM, nq), _BF16),
                        pltpu.VMEM((tk, nq), _F32),
                        pltpu.VMEM((tk, nq), _F32),
                        pltpu.VMEM((8, nq), _F32),
                        pltpu.VMEM((8, nq), _F32),
                        pltpu.VMEM((tk, nq), _BF16),
                        pltpu.VMEM((tk, nq), _BF16),
                        pltpu.VMEM((8, nq), _F32),
                        pltpu.VMEM((8, nq), _F32),
                        pltpu.VMEM((8, nq), _F32),
                        pltpu.VMEM((ACC_ROWS, nq), _F32)],
        compiler_params=pltpu.CompilerParams(dimension_semantics=("arbitrary",),
                                             vmem_limit_bytes=VMEM_LIMIT),
        name="attn",
    )(lam0, lp, gs, qt, k, vt)


def _mix_kernel(x_ref, a_ref, xp_ref, halo_ref, gate_ref, pw_ref, ps_ref,
                wua_ref, wup_ref, wo_ref, o_ref):
    tm = ROW_TILE
    i = pl.program_id(0)

    halo = jnp.where(i > 0, halo_ref[...], 0.0)
    ext = jnp.concatenate([halo, xp_ref[...]], axis=0)
    t1 = (lax.broadcasted_iota(jnp.int32, (tm, POOL_GROUP_DIM), 0) + (i * tm + 1)).astype(_F32)
    sums = ext
    pooled = []
    shift = 1
    for g, w in enumerate(POOL_WINDOWS):
        while shift < w:
            sums = sums + pltpu.roll(sums, shift, axis=0)
            shift *= 2
        lo = g * POOL_GROUP_DIM
        win = sums[POOL_HALO:, lo:lo + POOL_GROUP_DIM]
        cur = ext[POOL_HALO:, lo:lo + POOL_GROUP_DIM]
        pooled.append(win / jnp.minimum(t1, float(w)) - cur)
    mixed = [jnp.dot(pooled[g].astype(_BF16), pw_ref[g], preferred_element_type=_F32)
             for g in range(POOL_GROUPS)]
    p = (jnp.concatenate(mixed, axis=-1) * ps_ref[...]).astype(_BF16)

    y_attn = jnp.dot(a_ref[...], wua_ref[...], preferred_element_type=_F32)
    y_pool = jnp.dot(p, wup_ref[...], preferred_element_type=_F32)
    gates = gate_ref[...].astype(_F32)
    merged = gates[:, :D_MODEL] * y_attn + gates[:, D_MODEL:] * y_pool
    o_ref[...] = x_ref[...] + jnp.dot(merged.astype(_BF16), wo_ref[...], preferred_element_type=_F32)


def _mix(layer, x, a, xp, gates, pool_w, pool_scale, w_up_attn, w_up_pool, w_o):
    s = x.shape[0]
    tm = ROW_TILE
    row = lambda w: pl.BlockSpec((tm, w), lambda i: (i, 0))
    halo_blocks = tm // POOL_HALO
    halo = pl.BlockSpec((POOL_HALO, POOL_WIDTH), lambda i: (jnp.maximum(i * halo_blocks - 1, 0), 0))
    return pl.pallas_call(
        _mix_kernel,
        grid=(s // tm,),
        in_specs=[row(D_MODEL), row(ATTN_WIDTH), row(POOL_WIDTH), halo, row(GATE_WIDTH),
                  _layer_spec((POOL_GROUPS, POOL_GROUP_DIM, POOL_GROUP_DIM), layer),
                  _const_spec((1, POOL_WIDTH)),
                  _layer_spec((ATTN_WIDTH, D_MODEL), layer), _layer_spec((POOL_WIDTH, D_MODEL), layer),
                  _layer_spec((D_MODEL, D_MODEL), layer)],
        out_specs=row(D_MODEL),
        out_shape=jax.ShapeDtypeStruct((s, D_MODEL), _F32),
        compiler_params=pltpu.CompilerParams(dimension_semantics=("arbitrary",),
                                             vmem_limit_bytes=VMEM_LIMIT),
        name="mix",
    )(x, a, xp, xp, gates, pool_w, pool_scale, w_up_attn, w_up_pool, w_o)


def _mlp_kernel(x_ref, g2_ref, w1_ref, w2_ref, o_ref):
    x = x_ref[...]
    h = (x * _rms_scale(x) * g2_ref[...]).astype(_BF16)
    acc = x
    for c in range(D_FF // FF_CHUNK):
        lo = c * FF_CHUNK
        u = jnp.dot(h, w1_ref[:, lo:lo + FF_CHUNK], preferred_element_type=_F32)
        u = jnp.square(jnp.maximum(u, 0.0)).astype(_BF16)
        acc = acc + jnp.dot(u, w2_ref[lo:lo + FF_CHUNK, :], preferred_element_type=_F32)
    o_ref[...] = acc


def _mlp(layer, x, g2, w1, w2):
    s = x.shape[0]
    tm = ROW_TILE
    row = pl.BlockSpec((tm, D_MODEL), lambda i: (i, 0))
    return pl.pallas_call(
        _mlp_kernel,
        grid=(s // tm,),
        in_specs=[row, _const_spec((1, D_MODEL)), _layer_spec((D_MODEL, D_FF), layer),
                  _layer_spec((D_FF, D_MODEL), layer)],
        out_specs=row,
        out_shape=jax.ShapeDtypeStruct((s, D_MODEL), _F32),
        compiler_params=pltpu.CompilerParams(dimension_semantics=("arbitrary",),
                                             vmem_limit_bytes=VMEM_LIMIT),
        name="mlp",
    )(x, g2, w1, w2)


def _lambda_init(layer_idx):
    return 0.8 - 0.6 * math.exp(-0.3 * layer_idx)


def kernel(x, norm1_g, w_in, b_gate, q_norm_g, k_norm_g, lam_params, subln_g, pool_w, pool_scale,
           w_up_attn, w_up_pool, w_o, norm2_g, w_mlp_in, w_mlp_out):
    b, s, d = x.shape
    assert (b, d) == (1, D_MODEL) and s % ROW_TILE == 0 and s % ATTN_TQ == 0
    xs = x.reshape(s, d)
    lane_group = jnp.arange(QK_WIDTH) // HEAD_DIM
    seg = jnp.where(lane_group[:, None] == lane_group[None, :], 1.0 / HEAD_DIM, 0.0).astype(_BF16)
    n_rep = QK_WIDTH // HEAD_DIM
    w_in, pool_w, w_up_attn, w_up_pool, w_o, w_mlp_in, w_mlp_out = (
        w.astype(_BF16) for w in (w_in, pool_w, w_up_attn, w_up_pool, w_o, w_mlp_in, w_mlp_out))
    for l in range(DEPTH):
        gq = jnp.tile(q_norm_g[l], n_rep)[None, :] * (HEAD_DIM ** -0.5 * math.log2(math.e))
        gk = jnp.tile(k_norm_g[l], n_rep)[None, :]
        qt, k, vt, xp, gates = _proj(l, xs, norm1_g[l][None, :], w_in, b_gate[l][None, :], gq, gk, seg)
        lam0 = jnp.full((1, 128), _lambda_init(l), _F32)
        a = _attn(lam0, lam_params[l], subln_g[l][None, :], qt, k, vt)
        xs = _mix(l, xs, a, xp, gates, pool_w, pool_scale[l][None, :], w_up_attn, w_up_pool, w_o)
        xs = _mlp(l, xs, norm2_g[l][None, :], w_mlp_in, w_mlp_out)
    return xs.reshape(b, s, d)
```

```python
import functools
import math

import jax
import jax.numpy as jnp
from jax import lax
from jax.experimental import pallas as pl
from jax.experimental.pallas import tpu as pltpu

D_MODEL = 1024
DEPTH = 4
ATTN_HEADS = 4
HEAD_DIM = 64
V_DIM = 2 * HEAD_DIM
ATTN_WIDTH = ATTN_HEADS * V_DIM
QK_WIDTH = ATTN_HEADS * 2 * HEAD_DIM
POOL_WINDOWS = (2, 4, 8, 16)
POOL_GROUPS = len(POOL_WINDOWS)
POOL_GROUP_DIM = 128
POOL_WIDTH = POOL_GROUPS * POOL_GROUP_DIM
GATE_WIDTH = 2 * D_MODEL
IN_COLS = 2 * QK_WIDTH + ATTN_WIDTH + POOL_WIDTH + GATE_WIDTH
D_FF = 4 * D_MODEL
EPS = 1e-6
NEG_BIG = -1e30

POOL_HALO = 16
ROW_TILE = 1024
ATTN_TQ = 512
ATTN_TK = ATTN_TQ // 2
QK_ROWS = 128
MXU_COLS = 256
SM_ROWS = 32
SUM_ROWS = 16
ACC_ROWS = V_DIM + SUM_ROWS
PAIRS_PER_TRIP = 16
FF_CHUNK = 1024
VMEM_LIMIT = 56 * 1024 * 1024

_BF16 = jnp.bfloat16
_F32 = jnp.float32


def _const_spec(shape):
    return pl.BlockSpec(shape, lambda *_: (0,) * len(shape), pipeline_mode=pl.Buffered(1))


def _layer_spec(shape, layer):
    return pl.BlockSpec((None,) + shape, lambda *_: (layer,) + (0,) * len(shape),
                        pipeline_mode=pl.Buffered(1))


def _rms_scale(x):
    return lax.rsqrt(jnp.mean(x * x, axis=-1, keepdims=True) + EPS)


def _proj_kernel(x_ref, g1_ref, w_ref, bg_ref, gq_ref, gk_ref, seg_ref,
                 qt_ref, k_ref, vt_ref, xp_ref, gate_ref):
    x = x_ref[...]
    h = (x * _rms_scale(x) * g1_ref[...]).astype(_BF16)

    def cols(lo, hi):
        return jnp.dot(h, w_ref[:, lo:hi], preferred_element_type=_F32)

    def qk_norm(t, g_ref):
        ms = jnp.dot((t * t).astype(_BF16), seg_ref[...], preferred_element_type=_F32)
        return t * lax.rsqrt(ms + EPS) * g_ref[...]

    def store_head_major(dst_ref, t):
        for hd in range(ATTN_HEADS):
            dst_ref[hd * V_DIM:(hd + 1) * V_DIM, :] = t[:, hd * V_DIM:(hd + 1) * V_DIM].T.astype(_BF16)

    c_q, c_k = QK_WIDTH, 2 * QK_WIDTH
    c_v, c_p = c_k + ATTN_WIDTH, c_k + ATTN_WIDTH + POOL_WIDTH
    store_head_major(qt_ref, qk_norm(cols(0, c_q), gq_ref))
    k_ref[...] = qk_norm(cols(c_q, c_k), gk_ref).astype(_BF16)
    store_head_major(vt_ref, cols(c_k, c_v))
    xp_ref[...] = cols(c_v, c_p)
    gate_ref[...] = jax.nn.sigmoid(cols(c_p, IN_COLS) + bg_ref[...]).astype(_BF16)


def _proj(layer, x, g1, w_in, b_gate, gq, gk, seg):
    s = x.shape[0]
    tm = ROW_TILE
    row = lambda w: pl.BlockSpec((tm, w), lambda i: (i, 0))
    return pl.pallas_call(
        _proj_kernel,
        grid=(s // tm,),
        in_specs=[row(D_MODEL), _const_spec((1, D_MODEL)), _layer_spec((D_MODEL, IN_COLS), layer),
                  _const_spec((1, GATE_WIDTH)), _const_spec((1, QK_WIDTH)), _const_spec((1, QK_WIDTH)),
                  _const_spec((QK_WIDTH, QK_WIDTH))],
        out_specs=[pl.BlockSpec((QK_WIDTH, tm), lambda i: (0, i)), row(QK_WIDTH),
                   pl.BlockSpec((ATTN_WIDTH, tm), lambda i: (0, i)), row(POOL_WIDTH), row(GATE_WIDTH)],
        out_shape=[jax.ShapeDtypeStruct((QK_WIDTH, s), _BF16), jax.ShapeDtypeStruct((s, QK_WIDTH), _BF16),
                   jax.ShapeDtypeStruct((ATTN_WIDTH, s), _BF16), jax.ShapeDtypeStruct((s, POOL_WIDTH), _F32),
                   jax.ShapeDtypeStruct((s, GATE_WIDTH), _BF16)],
        compiler_params=pltpu.CompilerParams(dimension_semantics=("arbitrary",),
                                             vmem_limit_bytes=VMEM_LIMIT),
        name="proj",
    )(x, g1, w_in, b_gate, gq, gk, seg)


def _attn_kernel(*refs):
    n_blocks = refs[4].shape[0] // ATTN_TQ

    def block(i, carry):
        _attn_block(i, *refs)
        return carry

    lax.fori_loop(0, n_blocks, block, 0)


def _attn_block(i, lam0_ref, lp_ref, gs_ref, q_ref, k_ref, vt_ref, o_ref,
                qt_ref, s0_ref, s1_ref, mb0_ref, mb1_ref, p0_ref, p1_ref, al0_ref, al1_ref,
                m_ref, acc_ref):
    tq, tk = ATTN_TQ, ATTN_TK
    nq = 2 * tq
    s_slots, mb_slots = (s0_ref, s1_ref), (mb0_ref, mb1_ref)
    p_slots, al_slots = (p0_ref, p1_ref), (al0_ref, al1_ref)
    n_ctiles = nq // MXU_COLS
    n_sm = QK_ROWS // SM_ROWS

    _store_branch_qt(q_ref, qt_ref, i)
    m_ref[...] = jnp.full(m_ref.shape, NEG_BIG, _F32)
    acc_ref[...] = jnp.zeros(acc_ref.shape, _F32)
    p1_ref[...] = jnp.zeros(p1_ref.shape, _BF16)
    al1_ref[...] = jnp.ones(al1_ref.shape, _F32)

    def aligned(start, mult):
        return start if isinstance(start, int) else pl.multiple_of(start, mult)

    def col(c):
        return slice(c * MXU_COLS, (c + 1) * MXU_COLS)

    def qk_chunk(b, slot, g, c):
        row0 = g * QK_ROWS
        k = k_ref[pl.ds(aligned(b * tk + row0, QK_ROWS), QK_ROWS), :]
        s = jnp.dot(k, qt_ref[:, col(c)], preferred_element_type=_F32)
        s_slots[slot][row0:row0 + QK_ROWS, col(c)] = s
        cmax = jnp.max(s.reshape(QK_ROWS // 8, 8, MXU_COLS), axis=0)
        if g > 0:
            cmax = jnp.maximum(mb_slots[slot][:, col(c)], cmax)
        mb_slots[slot][:, col(c)] = cmax

    def qk_group(b, slot, g):
        return [functools.partial(qk_chunk, b, slot, g, c) for c in range(n_ctiles)]

    def pv_chunk(b, slot, c):
        vt = vt_ref[:, pl.ds(aligned(b * tk, tk), tk)]
        lhs = jnp.concatenate([vt, jnp.ones((SUM_ROWS, tk), _BF16)], axis=0)
        res = jnp.dot(lhs, p_slots[slot][:, col(c)], preferred_element_type=_F32)
        acc = acc_ref[:, col(c)].reshape(ACC_ROWS // 8, 8, MXU_COLS)
        acc = acc * al_slots[slot][:, col(c)][None] + res.reshape(ACC_ROWS // 8, 8, MXU_COLS)
        acc_ref[:, col(c)] = acc.reshape(ACC_ROWS, MXU_COLS)

    def pv_half(b, slot, half):
        per = n_ctiles // 2
        return [functools.partial(pv_chunk, b, slot, c) for c in range(half * per, (half + 1) * per)]

    def softmax_begin(slot, blk_max8):
        m_prev = m_ref[...]
        m_new = jnp.maximum(m_prev, jnp.max(blk_max8, axis=0, keepdims=True))
        alpha = jnp.exp2(m_prev - m_new)
        m_ref[...] = m_new
        al_slots[slot][...] = alpha

    def softmax_chunk(slot, r):
        s = s_slots[slot][r:r + SM_ROWS, :].reshape(SM_ROWS // 8, 8, nq)
        p = jnp.exp2(s - m_ref[...][None])
        p_slots[slot][r:r + SM_ROWS, :] = p.reshape(SM_ROWS, nq).astype(_BF16)

    def softmax_group(slot, g, mxu_work):
        n_w = len(mxu_work)
        for c in range(n_sm):
            for w in mxu_work[-(-c * n_w // n_sm):-(-(c + 1) * n_w // n_sm)]:
                w()
            softmax_chunk(slot, g * QK_ROWS + c * SM_ROWS)

    assert tk == MXU_COLS and tq == 2 * tk

    def diag_tiles(x, kinds):
        return [c for c in range(n_ctiles) if ("none", "tri", "all")[(c % 2 >= x) + (c % 2 > x)] in kinds]

    def mask_block(slot, x):
        bms = []
        for c in range(n_ctiles):
            if c in diag_tiles(x, ("none",)):
                bms.append(jnp.full((8, MXU_COLS), NEG_BIG, _F32))
                continue
            bm = None
            for r in range(0, tk, SM_ROWS):
                s = s_slots[slot][r:r + SM_ROWS, col(c)]
                if c in diag_tiles(x, ("tri",)):
                    key = lax.broadcasted_iota(jnp.int32, s.shape, 0) + r
                    s = jnp.where(key <= lax.broadcasted_iota(jnp.int32, s.shape, 1), s, NEG_BIG)
                    s_slots[slot][r:r + SM_ROWS, col(c)] = s
                cm = jnp.max(s.reshape(SM_ROWS // 8, 8, MXU_COLS), axis=0)
                bm = cm if bm is None else jnp.maximum(bm, cm)
            bms.append(bm)
        return jnp.concatenate(bms, axis=1)

    def softmax_tiles(slot, tiles):
        for r in range(0, tk, SM_ROWS):
            for c in tiles:
                s = s_slots[slot][r:r + SM_ROWS, col(c)].reshape(SM_ROWS // 8, 8, MXU_COLS)
                p = jnp.exp2(s - m_ref[:, col(c)][None])
                p_slots[slot][r:r + SM_ROWS, col(c)] = p.reshape(SM_ROWS, MXU_COLS).astype(_BF16)

    def pair_body(pair):
        b = 2 * pair
        pending = jnp.maximum(b - 3, 0)
        softmax_begin(0, mb0_ref[...])
        softmax_group(0, 0, qk_group(b - 1, 1, 1) + pv_half(pending, 1, 0))
        softmax_group(0, 1, qk_group(b, 0, 0) + pv_half(pending, 1, 1))
        softmax_begin(1, mb1_ref[...])
        softmax_group(1, 0, qk_group(b, 0, 1) + pv_half(b - 2, 0, 0))
        softmax_group(1, 1, qk_group(b + 1, 1, 0) + pv_half(b - 2, 0, 1))

    for w in qk_group(0, 0, 0) + qk_group(0, 0, 1) + qk_group(1, 1, 0):
        w()

    rem = i & (PAIRS_PER_TRIP - 1)
    bit = 1
    while bit < PAIRS_PER_TRIP:
        @pl.when((rem & bit) != 0)
        def _(bit=bit):
            for t in range(bit):
                pair_body(1 + (rem & (bit - 1)) + t)
        bit *= 2

    def body(j, carry):
        for t in range(PAIRS_PER_TRIP):
            pair_body(1 + rem + PAIRS_PER_TRIP * j + t)
        return carry

    lax.fori_loop(0, i // PAIRS_PER_TRIP, body, 0)

    b0 = 2 * i
    pending = jnp.maximum(b0 - 1, 0)
    for c in diag_tiles(1, ("tri", "all")):
        qk_chunk(b0 + 1, 1, 1, c)
    for w in pv_half(pending, 1, 0) + pv_half(pending, 1, 1):
        w()
    for x in range(2):
        live = diag_tiles(x, ("tri", "all"))
        softmax_begin(x, mask_block(x, x))
        softmax_tiles(x, live)
        for c in live:
            pv_chunk(b0 + x, x, c)

    _attn_finalize(lam0_ref, lp_ref, gs_ref, acc_ref, o_ref, i)


def _store_branch_qt(q_ref, qt_ref, i):
    tq = ATTN_TQ
    cols = pl.ds(pl.multiple_of(i * tq, tq), tq)
    zeros = jnp.zeros((HEAD_DIM, tq), _BF16)
    qt_ref[0:HEAD_DIM, 0:tq] = q_ref[0:HEAD_DIM, cols]
    qt_ref[HEAD_DIM:V_DIM, 0:tq] = zeros
    qt_ref[0:HEAD_DIM, tq:2 * tq] = zeros
    qt_ref[HEAD_DIM:V_DIM, tq:2 * tq] = q_ref[HEAD_DIM:V_DIM, cols]


def _attn_finalize(lam0_ref, lp_ref, gs_ref, acc_ref, o_ref, i):
    tq = ATTN_TQ
    lp = lp_ref[...]
    lam0 = lam0_ref[0:1, 0:1]
    lam = (jnp.exp(jnp.sum(lp[0:1, :] * lp[1:2, :], axis=-1, keepdims=True))
           - jnp.exp(jnp.sum(lp[2:3, :] * lp[3:4, :], axis=-1, keepdims=True)) + lam0)
    o = acc_ref[0:V_DIM, :] * (1.0 / acc_ref[V_DIM:V_DIM + 1, :])
    a = o[:, 0:tq] - lam * o[:, tq:2 * tq]
    a = a * lax.rsqrt(jnp.mean(a * a, axis=0, keepdims=True) + EPS)
    rows = pl.ds(pl.multiple_of(i * tq, tq), tq)
    o_ref[rows, :] = (a.T * gs_ref[...] * (1.0 - lam0)).astype(o_ref.dtype)


def _attn(lam0, lp, gs, qt, k, vt):
    s = k.shape[0]
    tq, tk = ATTN_TQ, ATTN_TK
    nq = 2 * tq
    return pl.pallas_call(
        _attn_kernel,
        grid=(ATTN_HEADS,),
        in_specs=[pl.BlockSpec((1, 128), lambda h: (0, 0)),
                  pl.BlockSpec((4, HEAD_DIM), lambda h: (0, 0)),
                  pl.BlockSpec((1, V_DIM), lambda h: (0, 0)),
                  pl.BlockSpec((V_DIM, s), lambda h: (h, 0)),
                  pl.BlockSpec((s, V_DIM), lambda h: (0, h)),
                  pl.BlockSpec((V_DIM, s), lambda h: (h, 0))],
        out_specs=pl.BlockSpec((s, V_DIM), lambda h: (0, h)),
        out_shape=jax.ShapeDtypeStruct((s, ATTN_WIDTH), _BF16),
        scratch_shapes=[pltpu.VMEM((V_DIM, nq), _BF16),
                        pltpu.VMEM((tk, nq), _F32),
                        pltpu.VMEM((tk, nq), _F32),
                        pltpu.VMEM((8, nq), _F32),
                        pltpu.VMEM((8, nq), _F32),
                        pltpu.VMEM((tk, nq), _BF16),
                        pltpu.VMEM((tk, nq), _BF16),
                        pltpu.VMEM((8, nq), _F32),
                        pltpu.VMEM((8, nq), _F32),
                        pltpu.VMEM((8, nq), _F32),
                        pltpu.VMEM((ACC_ROWS, nq), _F32)],
        compiler_params=pltpu.CompilerParams(dimension_semantics=("arbitrary",),
                                             vmem_limit_bytes=VMEM_LIMIT),
        name="attn",
    )(lam0, lp, gs, qt, k, vt)


def _mix_kernel(x_ref, a_ref, xp_ref, halo_ref, gate_ref, pw_ref, ps_ref,
                wua_ref, wup_ref, wo_ref, o_ref):
    tm = ROW_TILE
    i = pl.program_id(0)

    halo = jnp.where(i > 0, halo_ref[...], 0.0)
    ext = jnp.concatenate([halo, xp_ref[...]], axis=0)
    t1 = (lax.broadcasted_iota(jnp.int32, (tm, POOL_GROUP_DIM), 0) + (i * tm + 1)).astype(_F32)
    sums = ext
    pooled = []
    shift = 1
    for g, w in enumerate(POOL_WINDOWS):
        while shift < w:
            sums = sums + pltpu.roll(sums, shift, axis=0)
            shift *= 2
        lo = g * POOL_GROUP_DIM
        win = sums[POOL_HALO:, lo:lo + POOL_GROUP_DIM]
        cur = ext[POOL_HALO:, lo:lo + POOL_GROUP_DIM]
        pooled.append(win / jnp.minimum(t1, float(w)) - cur)
    mixed = [jnp.dot(pooled[g].astype(_BF16), pw_ref[g], preferred_element_type=_F32)
             for g in range(POOL_GROUPS)]
    p = (jnp.concatenate(mixed, axis=-1) * ps_ref[...]).astype(_BF16)

    y_attn = jnp.dot(a_ref[...], wua_ref[...], preferred_element_type=_F32)
    y_pool = jnp.dot(p, wup_ref[...], preferred_element_type=_F32)
    gates = gate_ref[...].astype(_F32)
    merged = gates[:, :D_MODEL] * y_attn + gates[:, D_MODEL:] * y_pool
    o_ref[...] = x_ref[...] + jnp.dot(merged.astype(_BF16), wo_ref[...], preferred_element_type=_F32)


def _mix(layer, x, a, xp, gates, pool_w, pool_scale, w_up_attn, w_up_pool, w_o):
    s = x.shape[0]
    tm = ROW_TILE
    row = lambda w: pl.BlockSpec((tm, w), lambda i: (i, 0))
    halo_blocks = tm // POOL_HALO
    halo = pl.BlockSpec((POOL_HALO, POOL_WIDTH), lambda i: (jnp.maximum(i * halo_blocks - 1, 0), 0))
    return pl.pallas_call(
        _mix_kernel,
        grid=(s // tm,),
        in_specs=[row(D_MODEL), row(ATTN_WIDTH), row(POOL_WIDTH), halo, row(GATE_WIDTH),
                  _layer_spec((POOL_GROUPS, POOL_GROUP_DIM, POOL_GROUP_DIM), layer),
                  _const_spec((1, POOL_WIDTH)),
                  _layer_spec((ATTN_WIDTH, D_MODEL), layer), _layer_spec((POOL_WIDTH, D_MODEL), layer),
                  _layer_spec((D_MODEL, D_MODEL), layer)],
        out_specs=row(D_MODEL),
        out_shape=jax.ShapeDtypeStruct((s, D_MODEL), _F32),
        compiler_params=pltpu.CompilerParams(dimension_semantics=("arbitrary",),
                                             vmem_limit_bytes=VMEM_LIMIT),
        name="mix",
    )(x, a, xp, xp, gates, pool_w, pool_scale, w_up_attn, w_up_pool, w_o)


def _mlp_kernel(x_ref, g2_ref, w1_ref, w2_ref, o_ref):
    x = x_ref[...]
    h = (x * _rms_scale(x) * g2_ref[...]).astype(_BF16)
    acc = x
    for c in range(D_FF // FF_CHUNK):
        lo = c * FF_CHUNK
        u = jnp.dot(h, w1_ref[:, lo:lo + FF_CHUNK], preferred_element_type=_F32)
        u = jnp.square(jnp.maximum(u, 0.0)).astype(_BF16)
        acc = acc + jnp.dot(u, w2_ref[lo:lo + FF_CHUNK, :], preferred_element_type=_F32)
    o_ref[...] = acc


def _mlp(layer, x, g2, w1, w2):
    s = x.shape[0]
    tm = ROW_TILE
    row = pl.BlockSpec((tm, D_MODEL), lambda i: (i, 0))
    return pl.pallas_call(
        _mlp_kernel,
        grid=(s // tm,),
        in_specs=[row, _const_spec((1, D_MODEL)), _layer_spec((D_MODEL, D_FF), layer),
                  _layer_spec((D_FF, D_MODEL), layer)],
        out_specs=row,
        out_shape=jax.ShapeDtypeStruct((s, D_MODEL), _F32),
        compiler_params=pltpu.CompilerParams(dimension_semantics=("arbitrary",),
                                             vmem_limit_bytes=VMEM_LIMIT),
        name="mlp",
    )(x, g2, w1, w2)


def _lambda_init(layer_idx):
    return 0.8 - 0.6 * math.exp(-0.3 * layer_idx)


def kernel(x, norm1_g, w_in, b_gate, q_norm_g, k_norm_g, lam_params, subln_g, pool_w, pool_scale,
           w_up_attn, w_up_pool, w_o, norm2_g, w_mlp_in, w_mlp_out):
    b, s, d = x.shape
    assert (b, d) == (1, D_MODEL) and s % ROW_TILE == 0 and s % ATTN_TQ == 0
    xs = x.reshape(s, d)
    lane_group = jnp.arange(QK_WIDTH) // HEAD_DIM
    seg = jnp.where(lane_group[:, None] == lane_group[None, :], 1.0 / HEAD_DIM, 0.0).astype(_BF16)
    n_rep = QK_WIDTH // HEAD_DIM
    w_in, pool_w, w_up_attn, w_up_pool, w_o, w_mlp_in, w_mlp_out = (
        w.astype(_BF16) for w in (w_in, pool_w, w_up_attn, w_up_pool, w_o, w_mlp_in, w_mlp_out))
    for l in range(DEPTH):
        gq = jnp.tile(q_norm_g[l], n_rep)[None, :] * (HEAD_DIM ** -0.5 * math.log2(math.e))
        gk = jnp.tile(k_norm_g[l], n_rep)[None, :]
        qt, k, vt, xp, gates = _proj(l, xs, norm1_g[l][None, :], w_in, b_gate[l][None, :], gq, gk, seg)
        lam0 = jnp.full((1, 128), _lambda_init(l), _F32)
        a = _attn(lam0, lam_params[l], subln_g[l][None, :], qt, k, vt)
        xs = _mix(l, xs, a, xp, gates, pool_w, pool_scale[l][None, :], w_up_attn, w_up_pool, w_o)
        xs = _mlp(l, xs, norm2_g[l][None, :], w_mlp_in, w_mlp_out)
    return xs.reshape(b, s, d)
```

```python
import functools
import math

import jax
import jax.numpy as jnp
from jax import lax
from jax.experimental import pallas as pl
from jax.experimental.pallas import tpu as pltpu

D_MODEL = 1024
DEPTH = 4
ATTN_HEADS = 4
HEAD_DIM = 64
V_DIM = 2 * HEAD_DIM
ATTN_WIDTH = ATTN_HEADS * V_DIM
QK_WIDTH = ATTN_HEADS * 2 * HEAD_DIM
POOL_WINDOWS = (2, 4, 8, 16)
POOL_GROUPS = len(POOL_WINDOWS)
POOL_GROUP_DIM = 128
POOL_WIDTH = POOL_GROUPS * POOL_GROUP_DIM
GATE_WIDTH = 2 * D_MODEL
IN_COLS = 2 * QK_WIDTH + ATTN_WIDTH + POOL_WIDTH + GATE_WIDTH
D_FF = 4 * D_MODEL
EPS = 1e-6
NEG_BIG = -1e30

POOL_HALO = 16
ROW_TILE = 1024
ATTN_TQ = 512
ATTN_TK = ATTN_TQ // 2
QK_ROWS = 128
MXU_COLS = 256
SM_ROWS = 64
SUM_ROWS = 16
ACC_ROWS = V_DIM + SUM_ROWS
PAIRS_PER_TRIP = 16
FF_CHUNK = 1024
VMEM_LIMIT = 56 * 1024 * 1024

_BF16 = jnp.bfloat16
_F32 = jnp.float32


def _const_spec(shape):
    return pl.BlockSpec(shape, lambda *_: (0,) * len(shape), pipeline_mode=pl.Buffered(1))


def _layer_spec(shape, layer):
    return pl.BlockSpec((None,) + shape, lambda *_: (layer,) + (0,) * len(shape),
                        pipeline_mode=pl.Buffered(1))


def _rms_scale(x):
    return lax.rsqrt(jnp.mean(x * x, axis=-1, keepdims=True) + EPS)


def _proj_kernel(x_ref, g1_ref, w_ref, bg_ref, gq_ref, gk_ref, seg_ref,
                 qt_ref, k_ref, vt_ref, xp_ref, gate_ref):
    x = x_ref[...]
    h = (x * _rms_scale(x) * g1_ref[...]).astype(_BF16)

    def cols(lo, hi):
        return jnp.dot(h, w_ref[:, lo:hi], preferred_element_type=_F32)

    def qk_norm(t, g_ref):
        ms = jnp.dot((t * t).astype(_BF16), seg_ref[...], preferred_element_type=_F32)
        return t * lax.rsqrt(ms + EPS) * g_ref[...]

    def store_head_major(dst_ref, t):
        for hd in range(ATTN_HEADS):
            dst_ref[hd * V_DIM:(hd + 1) * V_DIM, :] = t[:, hd * V_DIM:(hd + 1) * V_DIM].T.astype(_BF16)

    c_q, c_k = QK_WIDTH, 2 * QK_WIDTH
    c_v, c_p = c_k + ATTN_WIDTH, c_k + ATTN_WIDTH + POOL_WIDTH
    store_head_major(qt_ref, qk_norm(cols(0, c_q), gq_ref))
    k_ref[...] = qk_norm(cols(c_q, c_k), gk_ref).astype(_BF16)
    store_head_major(vt_ref, cols(c_k, c_v))
    xp_ref[...] = cols(c_v, c_p)
    gate_ref[...] = jax.nn.sigmoid(cols(c_p, IN_COLS) + bg_ref[...]).astype(_BF16)


def _proj(layer, x, g1, w_in, b_gate, gq, gk, seg):
    s = x.shape[0]
    tm = ROW_TILE
    row = lambda w: pl.BlockSpec((tm, w), lambda i: (i, 0))
    return pl.pallas_call(
        _proj_kernel,
        grid=(s // tm,),
        in_specs=[row(D_MODEL), _const_spec((1, D_MODEL)), _layer_spec((D_MODEL, IN_COLS), layer),
                  _const_spec((1, GATE_WIDTH)), _const_spec((1, QK_WIDTH)), _const_spec((1, QK_WIDTH)),
                  _const_spec((QK_WIDTH, QK_WIDTH))],
        out_specs=[pl.BlockSpec((QK_WIDTH, tm), lambda i: (0, i)), row(QK_WIDTH),
                   pl.BlockSpec((ATTN_WIDTH, tm), lambda i: (0, i)), row(POOL_WIDTH), row(GATE_WIDTH)],
        out_shape=[jax.ShapeDtypeStruct((QK_WIDTH, s), _BF16), jax.ShapeDtypeStruct((s, QK_WIDTH), _BF16),
                   jax.ShapeDtypeStruct((ATTN_WIDTH, s), _BF16), jax.ShapeDtypeStruct((s, POOL_WIDTH), _F32),
                   jax.ShapeDtypeStruct((s, GATE_WIDTH), _BF16)],
        compiler_params=pltpu.CompilerParams(dimension_semantics=("arbitrary",),
                                             vmem_limit_bytes=VMEM_LIMIT),
        name="proj",
    )(x, g1, w_in, b_gate, gq, gk, seg)


def _attn_kernel(*refs):
    n_blocks = refs[4].shape[0] // ATTN_TQ

    def block(i, carry):
        _attn_block(i, *refs)
        return carry

    lax.fori_loop(0, n_blocks, block, 0)


def _attn_block(i, lam0_ref, lp_ref, gs_ref, q_ref, k_ref, vt_ref, o_ref,
                qt_ref, s0_ref, s1_ref, mb0_ref, mb1_ref, p0_ref, p1_ref, al0_ref, al1_ref,
                m_ref, acc_ref):
    tq, tk = ATTN_TQ, ATTN_TK
    nq = 2 * tq
    s_slots, mb_slots = (s0_ref, s1_ref), (mb0_ref, mb1_ref)
    p_slots, al_slots = (p0_ref, p1_ref), (al0_ref, al1_ref)
    n_ctiles = nq // MXU_COLS
    n_sm = QK_ROWS // SM_ROWS

    _store_branch_qt(q_ref, qt_ref, i)
    m_ref[...] = jnp.full(m_ref.shape, NEG_BIG, _F32)
    acc_ref[...] = jnp.zeros(acc_ref.shape, _F32)
    p1_ref[...] = jnp.zeros(p1_ref.shape, _BF16)
    al1_ref[...] = jnp.ones(al1_ref.shape, _F32)

    def aligned(start, mult):
        return start if isinstance(start, int) else pl.multiple_of(start, mult)

    def col(c):
        return slice(c * MXU_COLS, (c + 1) * MXU_COLS)

    def qk_chunk(b, slot, g, c):
        row0 = g * QK_ROWS
        k = k_ref[pl.ds(aligned(b * tk + row0, QK_ROWS), QK_ROWS), :]
        s = jnp.dot(k, qt_ref[:, col(c)], preferred_element_type=_F32)
        s_slots[slot][row0:row0 + QK_ROWS, col(c)] = s
        cmax = jnp.max(s.reshape(QK_ROWS // 8, 8, MXU_COLS), axis=0)
        if g > 0:
            cmax = jnp.maximum(mb_slots[slot][:, col(c)], cmax)
        mb_slots[slot][:, col(c)] = cmax

    def qk_group(b, slot, g):
        return [functools.partial(qk_chunk, b, slot, g, c) for c in range(n_ctiles)]

    def pv_chunk(b, slot, c):
        vt = vt_ref[:, pl.ds(aligned(b * tk, tk), tk)]
        lhs = jnp.concatenate([vt, jnp.ones((SUM_ROWS, tk), _BF16)], axis=0)
        res = jnp.dot(lhs, p_slots[slot][:, col(c)], preferred_element_type=_F32)
        acc = acc_ref[:, col(c)].reshape(ACC_ROWS // 8, 8, MXU_COLS)
        acc = acc * al_slots[slot][:, col(c)][None] + res.reshape(ACC_ROWS // 8, 8, MXU_COLS)
        acc_ref[:, col(c)] = acc.reshape(ACC_ROWS, MXU_COLS)

    def pv_half(b, slot, half):
        per = n_ctiles // 2
        return [functools.partial(pv_chunk, b, slot, c) for c in range(half * per, (half + 1) * per)]

    def softmax_begin(slot, blk_max8):
        m_prev = m_ref[...]
        m_new = jnp.maximum(m_prev, jnp.max(blk_max8, axis=0, keepdims=True))
        alpha = jnp.exp2(m_prev - m_new)
        m_ref[...] = m_new
        al_slots[slot][...] = alpha

    def softmax_chunk(slot, r):
        s = s_slots[slot][r:r + SM_ROWS, :].reshape(SM_ROWS // 8, 8, nq)
        p = jnp.exp2(s - m_ref[...][None])
        p_slots[slot][r:r + SM_ROWS, :] = p.reshape(SM_ROWS, nq).astype(_BF16)

    def softmax_group(slot, g, mxu_work):
        n_w = len(mxu_work)
        for c in range(n_sm):
            for w in mxu_work[-(-c * n_w // n_sm):-(-(c + 1) * n_w // n_sm)]:
                w()
            softmax_chunk(slot, g * QK_ROWS + c * SM_ROWS)

    assert tk == MXU_COLS and tq == 2 * tk

    def diag_tiles(x, kinds):
        return [c for c in range(n_ctiles) if ("none", "tri", "all")[(c % 2 >= x) + (c % 2 > x)] in kinds]

    def mask_block(slot, x):
        bms = []
        for c in range(n_ctiles):
            if c in diag_tiles(x, ("none",)):
                bms.append(jnp.full((8, MXU_COLS), NEG_BIG, _F32))
                continue
            bm = None
            for r in range(0, tk, SM_ROWS):
                s = s_slots[slot][r:r + SM_ROWS, col(c)]
                if c in diag_tiles(x, ("tri",)):
                    key = lax.broadcasted_iota(jnp.int32, s.shape, 0) + r
                    s = jnp.where(key <= lax.broadcasted_iota(jnp.int32, s.shape, 1), s, NEG_BIG)
                    s_slots[slot][r:r + SM_ROWS, col(c)] = s
                cm = jnp.max(s.reshape(SM_ROWS // 8, 8, MXU_COLS), axis=0)
                bm = cm if bm is None else jnp.maximum(bm, cm)
            bms.append(bm)
        return jnp.concatenate(bms, axis=1)

    def softmax_tiles(slot, tiles):
        for r in range(0, tk, SM_ROWS):
            for c in tiles:
                s = s_slots[slot][r:r + SM_ROWS, col(c)].reshape(SM_ROWS // 8, 8, MXU_COLS)
                p = jnp.exp2(s - m_ref[:, col(c)][None])
                p_slots[slot][r:r + SM_ROWS, col(c)] = p.reshape(SM_ROWS, MXU_COLS).astype(_BF16)

    def pair_body(pair):
        b = 2 * pair
        pending = jnp.maximum(b - 3, 0)
        softmax_begin(0, mb0_ref[...])
        softmax_group(0, 0, qk_group(b - 1, 1, 1) + pv_half(pending, 1, 0))
        softmax_group(0, 1, qk_group(b, 0, 0) + pv_half(pending, 1, 1))
        softmax_begin(1, mb1_ref[...])
        softmax_group(1, 0, qk_group(b, 0, 1) + pv_half(b - 2, 0, 0))
        softmax_group(1, 1, qk_group(b + 1, 1, 0) + pv_half(b - 2, 0, 1))

    for w in qk_group(0, 0, 0) + qk_group(0, 0, 1) + qk_group(1, 1, 0):
        w()

    rem = i & (PAIRS_PER_TRIP - 1)
    bit = 1
    while bit < PAIRS_PER_TRIP:
        @pl.when((rem & bit) != 0)
        def _(bit=bit):
            for t in range(bit):
                pair_body(1 + (rem & (bit - 1)) + t)
        bit *= 2

    def body(j, carry):
        for t in range(PAIRS_PER_TRIP):
            pair_body(1 + rem + PAIRS_PER_TRIP * j + t)
        return carry

    lax.fori_loop(0, i // PAIRS_PER_TRIP, body, 0)

    b0 = 2 * i
    pending = jnp.maximum(b0 - 1, 0)
    for c in diag_tiles(1, ("tri", "all")):
        qk_chunk(b0 + 1, 1, 1, c)
    for w in pv_half(pending, 1, 0) + pv_half(pending, 1, 1):
        w()
    for x in range(2):
        live = diag_tiles(x, ("tri", "all"))
        softmax_begin(x, mask_block(x, x))
        softmax_tiles(x, live)
        for c in live:
            pv_chunk(b0 + x, x, c)

    _attn_finalize(lam0_ref, lp_ref, gs_ref, acc_ref, o_ref, i)


def _store_branch_qt(q_ref, qt_ref, i):
    tq = ATTN_TQ
    cols = pl.ds(pl.multiple_of(i * tq, tq), tq)
    zeros = jnp.zeros((HEAD_DIM, tq), _BF16)
    qt_ref[0:HEAD_DIM, 0:tq] = q_ref[0:HEAD_DIM, cols]
    qt_ref[HEAD_DIM:V_DIM, 0:tq] = zeros
    qt_ref[0:HEAD_DIM, tq:2 * tq] = zeros
    qt_ref[HEAD_DIM:V_DIM, tq:2 * tq] = q_ref[HEAD_DIM:V_DIM, cols]


def _attn_finalize(lam0_ref, lp_ref, gs_ref, acc_ref, o_ref, i):
    tq = ATTN_TQ
    lp = lp_ref[...]
    lam0 = lam0_ref[0:1, 0:1]
    lam = (jnp.exp(jnp.sum(lp[0:1, :] * lp[1:2, :], axis=-1, keepdims=True))
           - jnp.exp(jnp.sum(lp[2:3, :] * lp[3:4, :], axis=-1, keepdims=True)) + lam0)
    o = acc_ref[0:V_DIM, :] * (1.0 / acc_ref[V_DIM:V_DIM + 1, :])
    a = o[:, 0:tq] - lam * o[:, tq:2 * tq]
    a = a * lax.rsqrt(jnp.mean(a * a, axis=0, keepdims=True) + EPS)
    rows = pl.ds(pl.multiple_of(i * tq, tq), tq)
    o_ref[rows, :] = (a.T * gs_ref[...] * (1.0 - lam0)).astype(o_ref.dtype)


def _attn(lam0, lp, gs, qt, k, vt):
    s = k.shape[0]
    tq, tk = ATTN_TQ, ATTN_TK
    nq = 2 * tq
    return pl.pallas_call(
        _attn_kernel,
        grid=(ATTN_HEADS,),
        in_specs=[pl.BlockSpec((1, 128), lambda h: (0, 0)),
                  pl.BlockSpec((4, HEAD_DIM), lambda h: (0, 0)),
                  pl.BlockSpec((1, V_DIM), lambda h: (0, 0)),
                  pl.BlockSpec((V_DIM, s), lambda h: (h, 0)),
                  pl.BlockSpec((s, V_DIM), lambda h: (0, h)),
                  pl.BlockSpec((V_DIM, s), lambda h: (h, 0))],
        out_specs=pl.BlockSpec((s, V_DIM), lambda h: (0, h)),
        out_shape=jax.ShapeDtypeStruct((s, ATTN_WIDTH), _BF16),
        scratch_shapes=[pltpu.VMEM((V_DIM, nq), _BF16),
                        pltpu.VMEM((tk, nq), _F32),
                        pltpu.VMEM((tk, nq), _F32),
                        pltpu.VMEM((8, nq), _F32),
                        pltpu.VMEM((8, nq), _F32),
                        pltpu.VMEM((tk, nq), _BF16),
                        pltpu.VMEM((tk, nq), _BF16),
                        pltpu.VMEM((8, nq), _F32),
                        pltpu.VMEM((8, nq), _F32),
                        pltpu.VMEM((8, nq), _F32),
                        pltpu.VMEM((ACC_ROWS, nq), _F32)],
        compiler_params=pltpu.CompilerParams(dimension_semantics=("arbitrary",),
                                             vmem_limit_bytes=VMEM_LIMIT),
        name="attn",
    )(lam0, lp, gs, qt, k, vt)


def _mix_kernel(x_ref, a_ref, xp_ref, halo_ref, gate_ref, pw_ref, ps_ref,
                wua_ref, wup_ref, wo_ref, o_ref):
    tm = ROW_TILE
    i = pl.program_id(0)

    halo = jnp.where(i > 0, halo_ref[...], 0.0)
    ext = jnp.concatenate([halo, xp_ref[...]], axis=0)
    t1 = (lax.broadcasted_iota(jnp.int32, (tm, POOL_GROUP_DIM), 0) + (i * tm + 1)).astype(_F32)
    sums = ext
    pooled = []
    shift = 1
    for g, w in enumerate(POOL_WINDOWS):
        while shift < w:
            sums = sums + pltpu.roll(sums, shift, axis=0)
            shift *= 2
        lo = g * POOL_GROUP_DIM
        win = sums[POOL_HALO:, lo:lo + POOL_GROUP_DIM]
        cur = ext[POOL_HALO:, lo:lo + POOL_GROUP_DIM]
        pooled.append(win / jnp.minimum(t1, float(w)) - cur)
    mixed = [jnp.dot(pooled[g].astype(_BF16), pw_ref[g], preferred_element_type=_F32)
             for g in range(POOL_GROUPS)]
    p = (jnp.concatenate(mixed, axis=-1) * ps_ref[...]).astype(_BF16)

    y_attn = jnp.dot(a_ref[...], wua_ref[...], preferred_element_type=_F32)
    y_pool = jnp.dot(p, wup_ref[...], preferred_element_type=_F32)
    gates = gate_ref[...].astype(_F32)
    merged = gates[:, :D_MODEL] * y_attn + gates[:, D_MODEL:] * y_pool
    o_ref[...] = x_ref[...] + jnp.dot(merged.astype(_BF16), wo_ref[...], preferred_element_type=_F32)


def _mix(layer, x, a, xp, gates, pool_w, pool_scale, w_up_attn, w_up_pool, w_o):
    s = x.shape[0]
    tm = ROW_TILE
    row = lambda w: pl.BlockSpec((tm, w), lambda i: (i, 0))
    halo_blocks = tm // POOL_HALO
    halo = pl.BlockSpec((POOL_HALO, POOL_WIDTH), lambda i: (jnp.maximum(i * halo_blocks - 1, 0), 0))
    return pl.pallas_call(
        _mix_kernel,
        grid=(s // tm,),
        in_specs=[row(D_MODEL), row(ATTN_WIDTH), row(POOL_WIDTH), halo, row(GATE_WIDTH),
                  _layer_spec((POOL_GROUPS, POOL_GROUP_DIM, POOL_GROUP_DIM), layer),
                  _const_spec((1, POOL_WIDTH)),
                  _layer_spec((ATTN_WIDTH, D_MODEL), layer), _layer_spec((POOL_WIDTH, D_MODEL), layer),
                  _layer_spec((D_MODEL, D_MODEL), layer)],
        out_specs=row(D_MODEL),
        out_shape=jax.ShapeDtypeStruct((s, D_MODEL), _F32),
        compiler_params=pltpu.CompilerParams(dimension_semantics=("arbitrary",),
                                             vmem_limit_bytes=VMEM_LIMIT),
        name="mix",
    )(x, a, xp, xp, gates, pool_w, pool_scale, w_up_attn, w_up_pool, w_o)


def _mlp_kernel(x_ref, g2_ref, w1_ref, w2_ref, o_ref):
    x = x_ref[...]
    h = (x * _rms_scale(x) * g2_ref[...]).astype(_BF16)
    acc = x
    for c in range(D_FF // FF_CHUNK):
        lo = c * FF_CHUNK
        u = jnp.dot(h, w1_ref[:, lo:lo + FF_CHUNK], preferred_element_type=_F32)
        u = jnp.square(jnp.maximum(u, 0.0)).astype(_BF16)
        acc = acc + jnp.dot(u, w2_ref[lo:lo + FF_CHUNK, :], preferred_element_type=_F32)
    o_ref[...] = acc


def _mlp(layer, x, g2, w1, w2):
    s = x.shape[0]
    tm = ROW_TILE
    row = pl.BlockSpec((tm, D_MODEL), lambda i: (i, 0))
    return pl.pallas_call(
        _mlp_kernel,
        grid=(s // tm,),
        in_specs=[row, _const_spec((1, D_MODEL)), _layer_spec((D_MODEL, D_FF), layer),
                  _layer_spec((D_FF, D_MODEL), layer)],
        out_specs=row,
        out_shape=jax.ShapeDtypeStruct((s, D_MODEL), _F32),
        compiler_params=pltpu.CompilerParams(dimension_semantics=("arbitrary",),
                                             vmem_limit_bytes=VMEM_LIMIT),
        name="mlp",
    )(x, g2, w1, w2)


def _lambda_init(layer_idx):
    return 0.8 - 0.6 * math.exp(-0.3 * layer_idx)


def kernel(x, norm1_g, w_in, b_gate, q_norm_g, k_norm_g, lam_params, subln_g, pool_w, pool_scale,
           w_up_attn, w_up_pool, w_o, norm2_g, w_mlp_in, w_mlp_out):
    b, s, d = x.shape
    assert (b, d) == (1, D_MODEL) and s % ROW_TILE == 0 and s % ATTN_TQ == 0
    xs = x.reshape(s, d)
    lane_group = jnp.arange(QK_WIDTH) // HEAD_DIM
    seg = jnp.where(lane_group[:, None] == lane_group[None, :], 1.0 / HEAD_DIM, 0.0).astype(_BF16)
    n_rep = QK_WIDTH // HEAD_DIM
    w_in, pool_w, w_up_attn, w_up_pool, w_o, w_mlp_in, w_mlp_out = (
        w.astype(_BF16) for w in (w_in, pool_w, w_up_attn, w_up_pool, w_o, w_mlp_in, w_mlp_out))
    for l in range(DEPTH):
        gq = jnp.tile(q_norm_g[l], n_rep)[None, :] * (HEAD_DIM ** -0.5 * math.log2(math.e))
        gk = jnp.tile(k_norm_g[l], n_rep)[None, :]
        qt, k, vt, xp, gates = _proj(l, xs, norm1_g[l][None, :], w_in, b_gate[l][None, :], gq, gk, seg)
        lam0 = jnp.full((1, 128), _lambda_init(l), _F32)
        a = _attn(lam0, lam_params[l], subln_g[l][None, :], qt, k, vt)
        xs = _mix(l, xs, a, xp, gates, pool_w, pool_scale[l][None, :], w_up_attn, w_up_pool, w_o)
        xs = _mlp(l, xs, norm2_g[l][None, :], w_mlp_in, w_mlp_out)
    return xs.reshape(b, s, d)
```
